```python
import math
import jax
import jax.numpy as jnp
from jax import lax
import numpy as np

D_MODEL = 1024
BATCH = 4
SEQ = 4096
DEPTH = 2

HEAD_DIM = 64
Q_BLOCK = 128
SB_HEADS = 8
NSA_HEADS = 8
NSA_KV_GROUPS = 2
CMP_BLOCK = 32
CMP_STRIDE = 16
CMP_HIDDEN = 128
SLC_BLOCK = 64
SLC_TOPN = 16
WINDOW = 512
FORCE_SCORE = 1e6
SB_W = SB_HEADS * HEAD_DIM
NSA_QW = NSA_HEADS * HEAD_DIM
NSA_KVW = NSA_KV_GROUPS * HEAD_DIM
ATTN_IN = 3 * SB_W + NSA_QW + 6 * NSA_KVW + 3 * NSA_HEADS
ATTN_OUT = SB_W + NSA_QW
SSM_EXPAND = 2
SSM_INNER = SSM_EXPAND * D_MODEL
SSM_HEAD_DIM = 64
SSM_HEADS = SSM_INNER // SSM_HEAD_DIM
SSM_GROUPS = 4
SSM_STATE = 128
SSM_CONV = 4
SSM_CHUNK = 256
SSM_CONV_DIM = SSM_INNER + 2 * SSM_GROUPS * SSM_STATE
SSM_IN = SSM_INNER + SSM_CONV_DIM + SSM_HEADS
N_EXPERTS = 32
TOP_K = 4
D_EXPERT = D_MODEL
SWIGLU_LIMIT = 7.0
SWIGLU_ALPHA = 1.702
MOE_BLOCK = 256
EPS = 1e-6

kernel_name = 'hybrid_sb_nsa_ssd_moe_block'


def rms_norm(x, gain):
    xf = x.astype(jnp.float32)
    y = xf * lax.rsqrt(jnp.mean(xf * xf, axis=-1, keepdims=True) + EPS)
    return (y * gain.astype(jnp.float32)).astype(x.dtype)


def alibi_slopes(n):
    return jnp.asarray(2.0 ** (-8.0 * np.arange(1, n + 1) / n), jnp.float32)


def masked_softmax(s, mask):
    s = jnp.where(mask, s, -jnp.inf)
    m = jnp.max(s, axis=-1, keepdims=True)
    m = jnp.where(jnp.isfinite(m), m, 0.0)
    p = jnp.exp(s - m)
    return p / jnp.maximum(jnp.sum(p, axis=-1, keepdims=True), 1e-30)


def stick_breaking_attention(q, k, v):
    b, s, h, dh = q.shape
    scale = dh ** -0.5
    outs = []
    for i in range(s // Q_BLOCK):
        q0, end = i * Q_BLOCK, (i + 1) * Q_BLOCK
        logits = jnp.einsum('bqhd,bkhd->bhqk', q[:, q0:end], k[:, :end]).astype(jnp.float32) * scale
        causal = jnp.arange(end)[None, :] < jnp.arange(q0, end)[:, None]
        log_beta = jax.nn.log_sigmoid(logits)
        log_keep = jnp.where(causal, jax.nn.log_sigmoid(-logits), 0.0)
        acc = lax.cumsum(log_keep, axis=3, reverse=True) - log_keep
        w = jnp.where(causal, jnp.exp(log_beta + acc), 0.0)
        outs.append(jnp.einsum('bhqk,bkhd->bqhd', w, v[:, :end]))
    return jnp.concatenate(outs, axis=1)


def compress_blocks(kv, pe, w1, w2):
    b, s, g, dh = kv.shape
    n_cmp = (s - CMP_BLOCK) // CMP_STRIDE + 1
    idx = np.arange(n_cmp)[:, None] * CMP_STRIDE + np.arange(CMP_BLOCK)[None, :]
    blocks = kv[:, idx] + pe[:, None, :]
    flat = jnp.swapaxes(blocks, 2, 3).reshape(b, n_cmp, g, CMP_BLOCK * dh)
    return jax.nn.silu(flat @ w1) @ w2


def nsa_attention(q, kc, vc, ks, vs, kw, vw, gates, q_norm, k_norm, pe_k, pe_v, w1_k, w2_k, w1_v, w2_v):
    b, s, h, dh = q.shape
    g = NSA_KV_GROUPS
    r = h // g
    scale = dh ** -0.5
    slopes = alibi_slopes(h).reshape(g, r)
    qg = (rms_norm(q, q_norm).astype(jnp.float32) * scale).reshape(b, s, g, r, dh)
    tpos = jnp.arange(s)

    k_cmp = rms_norm(compress_blocks(kc, pe_k, w1_k, w2_k), k_norm[0])
    v_cmp = compress_blocks(vc, pe_v, w1_v, w2_v)
    n_cmp = k_cmp.shape[1]
    cmp_start = np.arange(n_cmp) * CMP_STRIDE
    cmp_end = jnp.asarray(cmp_start + CMP_BLOCK - 1)
    mask_c = cmp_end[None, :] <= tpos[:, None]
    dist_c = (tpos[:, None] - cmp_end[None, :]).astype(jnp.float32)
    sc = jnp.einsum('bsgrd,bngd->bgrsn', qg, k_cmp) - slopes[:, :, None, None] * dist_c
    p_c = masked_softmax(sc, mask_c)
    o_c = jnp.einsum('bgrsn,bngd->bsgrd', p_c, v_cmp)

    n_slc = s // SLC_BLOCK
    n_sel = min(SLC_TOPN, n_slc)
    slc_start = np.arange(n_slc) * SLC_BLOCK
    overlap = ((cmp_start[:, None] < slc_start[None, :] + SLC_BLOCK)
               & (cmp_start[:, None] + CMP_BLOCK > slc_start[None, :])).astype(np.float32)
    p_slc = jnp.einsum('bgsn,nj->bgsj', jnp.sum(p_c, axis=2), jnp.asarray(overlap))
    jblk = jnp.arange(n_slc)
    valid = jblk[None, :] * SLC_BLOCK <= tpos[:, None]
    cur = tpos[:, None] // SLC_BLOCK
    forced = (jblk[None, :] == 0) | (jblk[None, :] == cur) | (jblk[None, :] == cur - 1)
    score = jnp.where(valid, jnp.where(forced, FORCE_SCORE, p_slc), -FORCE_SCORE)
    _, sel = lax.top_k(score, n_sel)

    k_sb = jnp.transpose(rms_norm(ks, k_norm[1]).reshape(b, n_slc, SLC_BLOCK, g, dh), (0, 3, 1, 2, 4))
    v_sb = jnp.transpose(vs.reshape(b, n_slc, SLC_BLOCK, g, dh), (0, 3, 1, 2, 4))
    nc = s // Q_BLOCK
    bi = jnp.arange(b)[:, None, None, None]
    gi = jnp.arange(g)[None, :, None, None]
    offs = jnp.arange(SLC_BLOCK)

    def selected_block(args):
        qc, ic, tc = args
        kb = k_sb[bi, gi, ic]
        vb = v_sb[bi, gi, ic]
        pos = ic[..., None] * SLC_BLOCK + offs
        tq = tc[None, None, :, None, None]
        mask = (pos <= tq)[:, :, None]
        dist = (tq - pos).astype(jnp.float32)[:, :, None]
        ss = jnp.einsum('bqgrd,bgqnld->bgrqnl', qc, kb) - slopes[None, :, :, None, None, None] * dist
        shp = ss.shape
        p = masked_softmax(ss.reshape(shp[:4] + (-1,)), jnp.broadcast_to(mask, shp).reshape(shp[:4] + (-1,)))
        return jnp.einsum('bgrqnl,bgqnld->bqgrd', p.reshape(shp), vb)

    q_ch = jnp.swapaxes(qg.reshape(b, nc, Q_BLOCK, g, r, dh), 0, 1)
    i_ch = jnp.transpose(sel.reshape(b, g, nc, Q_BLOCK, n_sel), (2, 0, 1, 3, 4))
    t_ch = jnp.arange(s, dtype=jnp.int32).reshape(nc, Q_BLOCK)
    o_s = jnp.swapaxes(lax.map(selected_block, (q_ch, i_ch, t_ch)), 0, 1).reshape(b, s, g, r, dh)

    k_w = jnp.pad(rms_norm(kw, k_norm[2]), ((0, 0), (WINDOW, 0), (0, 0), (0, 0)))
    v_w = jnp.pad(vw, ((0, 0), (WINDOW, 0), (0, 0), (0, 0)))
    widx = np.arange(nc)[:, None] * Q_BLOCK + np.arange(Q_BLOCK + WINDOW)[None, :]
    spos = widx - WINDOW
    tq = np.arange(nc)[:, None] * Q_BLOCK + np.arange(Q_BLOCK)[None, :]
    mask_w = ((spos[:, None, :] <= tq[:, :, None]) & (spos[:, None, :] > tq[:, :, None] - WINDOW)
              & (spos[:, None, :] >= 0))
    dist_w = jnp.asarray((tq[:, :, None] - spos[:, None, :]).astype(np.float32))
    qb = qg.reshape(b, nc, Q_BLOCK, g, r, dh)
    sw = jnp.einsum('bnqgrd,bnkgd->bgrnqk', qb, k_w[:, widx]) - slopes[:, :, None, None, None] * dist_w
    p_w = masked_softmax(sw, jnp.asarray(mask_w))
    o_w = jnp.einsum('bgrnqk,bnkgd->bnqgrd', p_w, v_w[:, widx]).reshape(b, s, g, r, dh)

    gt = jax.nn.sigmoid(gates.astype(jnp.float32)).reshape(b, s, g, r, 3, 1)
    o = gt[..., 0, :] * o_c + gt[..., 1, :] * o_s + gt[..., 2, :] * o_w
    return o.reshape(b, s, h, dh)


def attention_mixer(u, w_in, w_out, q_norm, k_norm, pe_k, pe_v, w1_k, w2_k, w1_v, w2_v):
    b, s, _ = u.shape
    proj = u @ w_in
    splits = [int(v) for v in np.cumsum([SB_W, SB_W, SB_W, NSA_QW] + [NSA_KVW] * 6)]
    sb_q, sb_k, sb_v, nq, kc, vc, ks, vs, kw, vw, gates = jnp.split(proj, splits, axis=-1)
    hs = lambda a, n: a.reshape(b, s, n, HEAD_DIM)
    o_sb = stick_breaking_attention(hs(sb_q, SB_HEADS), hs(sb_k, SB_HEADS), hs(sb_v, SB_HEADS))
    g = NSA_KV_GROUPS
    o_nsa = nsa_attention(hs(nq, NSA_HEADS), hs(kc, g), hs(vc, g), hs(ks, g), hs(vs, g), hs(kw, g), hs(vw, g),
                          gates, q_norm, k_norm, pe_k, pe_v, w1_k, w2_k, w1_v, w2_v)
    o = jnp.concatenate([o_sb.reshape(b, s, SB_W), o_nsa.reshape(b, s, NSA_QW)], axis=-1).astype(u.dtype)
    return o @ w_out


def ssd_chunked(x, dt, a, bm, cm):
    b, s, h, p = x.shape
    g, n = bm.shape[-2], bm.shape[-1]
    r = h // g
    chunk = math.gcd(s, SSM_CHUNK)
    nc = s // chunk
    xd = (x.astype(jnp.float32) * dt[..., None]).reshape(b, nc, chunk, g, r, p)
    cs = jnp.cumsum((dt * a).reshape(b, nc, chunk, g, r), axis=2)
    bm = bm.astype(jnp.float32).reshape(b, nc, chunk, g, n)
    cm = cm.astype(jnp.float32).reshape(b, nc, chunk, g, n)
    causal = jnp.tril(jnp.ones((chunk, chunk), bool))[None, None, :, :, None, None]
    seg = jnp.where(causal, cs[:, :, :, None] - cs[:, :, None, :], -jnp.inf)
    cb = jnp.einsum('bclgn,bcsgn->bclsg', cm, bm)
    y_diag = jnp.einsum('bclsgr,bcsgrp->bclgrp', cb[..., None] * jnp.exp(seg), xd)
    states = jnp.einsum('bclgn,bclgrp->bcgrpn', bm, xd * jnp.exp(cs[:, :, -1:] - cs)[..., None])
    chunk_decay = jnp.exp(cs[:, :, -1])

    def carry_state(hstate, inp):
        st, dec = inp
        return hstate * dec[..., None, None] + st, hstate

    _, prev = lax.scan(carry_state, jnp.zeros((b, g, r, p, n), jnp.float32),
                       (jnp.moveaxis(states, 1, 0), jnp.moveaxis(chunk_decay, 1, 0)))
    prev = jnp.moveaxis(prev, 0, 1)
    y_off = jnp.einsum('bclgn,bcgrpn->bclgrp', cm, prev) * jnp.exp(cs)[..., None]
    return (y_diag + y_off).reshape(b, s, h, p)


def mamba2_mixer(u, w_in, conv_w, conv_b, dt_bias, a_log, d_skip, norm_g, w_out):
    b, s, _ = u.shape
    zxbcdt = u @ w_in
    z, xbc, dt = jnp.split(zxbcdt, [SSM_INNER, SSM_INNER + SSM_CONV_DIM], axis=-1)
    xbc = lax.conv_general_dilated(xbc, conv_w[:, None, :], window_strides=(1,), padding=[(SSM_CONV - 1, 0)],
                                   dimension_numbers=('NWC', 'WIO', 'NWC'), feature_group_count=SSM_CONV_DIM)
    xbc = jax.nn.silu(xbc + conv_b)
    xs, bm, cm = jnp.split(xbc, [SSM_INNER, SSM_INNER + SSM_GROUPS * SSM_STATE], axis=-1)
    xs = xs.reshape(b, s, SSM_HEADS, SSM_HEAD_DIM)
    bm = bm.reshape(b, s, SSM_GROUPS, SSM_STATE)
    cm = cm.reshape(b, s, SSM_GROUPS, SSM_STATE)
    dt = jax.nn.softplus(dt.astype(jnp.float32) + dt_bias.astype(jnp.float32))
    a = -jnp.exp(a_log.astype(jnp.float32))
    y = ssd_chunked(xs, dt, a, bm, cm) + xs.astype(jnp.float32) * d_skip.astype(jnp.float32)[:, None]
    y = y.reshape(b, s, SSM_INNER) * jax.nn.silu(z.astype(jnp.float32))
    y = rms_norm(y.reshape(b, s, SSM_GROUPS, SSM_INNER // SSM_GROUPS), jnp.ones((), jnp.float32))
    y = (y.reshape(b, s, SSM_INNER) * norm_g).astype(u.dtype)
    return y @ w_out


def moe_ffn(h, w_router, b_router, w_gu, b_gu, w_down, b_down):
    b, s, d = h.shape
    n_tok = b * s
    tok = h.reshape(n_tok, d)
    logits = (tok @ w_router + b_router).astype(jnp.float32)
    top_logit, top_e = lax.top_k(logits, TOP_K)
    top_w = jax.nn.softmax(top_logit, axis=-1)
    n_rows = n_tok * TOP_K
    flat_e = top_e.reshape(n_rows)
    flat_t = jnp.repeat(jnp.arange(n_tok, dtype=jnp.int32), TOP_K)
    flat_w = top_w.reshape(n_rows)
    order = jnp.argsort(flat_e)
    sorted_e = flat_e[order]
    counts = jnp.bincount(flat_e, length=N_EXPERTS)
    padded = (counts + MOE_BLOCK - 1) // MOE_BLOCK * MOE_BLOCK
    pad_end = jnp.cumsum(padded)
    pad_start = pad_end - padded
    grp_start = jnp.cumsum(counts) - counts
    slot = pad_start[sorted_e] + jnp.arange(n_rows) - grp_start[sorted_e]
    n_slots = (n_rows + MOE_BLOCK - 1) // MOE_BLOCK * MOE_BLOCK + N_EXPERTS * MOE_BLOCK
    slot_tok = jnp.full((n_slots,), n_tok, jnp.int32).at[slot].set(flat_t[order])
    slot_w = jnp.zeros((n_slots,), jnp.float32).at[slot].set(flat_w[order])
    n_blocks = n_slots // MOE_BLOCK
    blk_e = jnp.minimum(jnp.searchsorted(pad_end, jnp.arange(n_blocks) * MOE_BLOCK, side='right'), N_EXPERTS - 1)
    tok_pad = jnp.concatenate([tok, jnp.zeros((1, d), tok.dtype)], axis=0)
    xs = tok_pad[slot_tok].reshape(n_blocks, MOE_BLOCK, d)

    def expert_block(args):
        xb, e = args
        gu = xb @ w_gu[e] + b_gu[e]
        gate = jnp.minimum(gu[:, :D_EXPERT], SWIGLU_LIMIT)
        up = jnp.clip(gu[:, D_EXPERT:], -SWIGLU_LIMIT, SWIGLU_LIMIT)
        act = (up + 1.0) * gate * jax.nn.sigmoid(SWIGLU_ALPHA * gate)
        return act @ w_down[e] + b_down[e]

    ys = lax.map(expert_block, (xs, blk_e)).reshape(n_slots, d) * slot_w[:, None]
    out = jax.ops.segment_sum(ys, slot_tok, num_segments=n_tok + 1)[:n_tok]
    return out.reshape(b, s, d).astype(h.dtype)


def setup_inputs(seed: int = 0) -> dict:
    key = jax.random.key(seed)
    ks = iter(jax.random.split(key, 40))
    nrm = lambda shape, sc: jax.random.normal(next(ks), shape, jnp.float32) * sc
    D = D_MODEL
    ne, no = (DEPTH + 1) // 2, DEPTH // 2
    fk = CMP_BLOCK * HEAD_DIM
    dt0 = jnp.exp(jax.random.uniform(next(ks), (no, SSM_HEADS), jnp.float32, math.log(1e-3), math.log(1e-1)))
    return {
        'x': nrm((BATCH, SEQ, D), 1.0),
        'c': nrm((BATCH, D), 1.0),
        'ada_w': nrm((DEPTH, D, 6 * D), 0.5 * D ** -0.5),
        'ada_b': nrm((DEPTH, 6 * D), 0.05),
        'norm_mix': 1.0 + nrm((DEPTH, D), 0.05),
        'norm_ffn': 1.0 + nrm((DEPTH, D), 0.05),
        'attn_w_in': nrm((ne, D, ATTN_IN), D ** -0.5),
        'attn_w_out': nrm((ne, ATTN_OUT, D), ATTN_OUT ** -0.5),
        'nsa_q_norm': 1.0 + nrm((ne, HEAD_DIM), 0.05),
        'nsa_k_norm': 1.0 + nrm((ne, 3, HEAD_DIM), 0.05),
        'cmp_pe_k': nrm((ne, CMP_BLOCK, HEAD_DIM), 0.1),
        'cmp_pe_v': nrm((ne, CMP_BLOCK, HEAD_DIM), 0.1),
        'cmp_w1_k': nrm((ne, fk, CMP_HIDDEN), fk ** -0.5),
        'cmp_w2_k': nrm((ne, CMP_HIDDEN, HEAD_DIM), CMP_HIDDEN ** -0.5),
        'cmp_w1_v': nrm((ne, fk, CMP_HIDDEN), fk ** -0.5),
        'cmp_w2_v': nrm((ne, CMP_HIDDEN, HEAD_DIM), CMP_HIDDEN ** -0.5),
        'ssm_w_in': nrm((no, D, SSM_IN), D ** -0.5),
        'ssm_conv_w': nrm((no, SSM_CONV, SSM_CONV_DIM), SSM_CONV ** -0.5),
        'ssm_conv_b': nrm((no, SSM_CONV_DIM), 0.02),
        'ssm_dt_bias': dt0 + jnp.log(-jnp.expm1(-dt0)),
        'ssm_a_log': jnp.log(jax.random.uniform(next(ks), (no, SSM_HEADS), jnp.float32, 1.0, 16.0)),
        'ssm_d': 1.0 + nrm((no, SSM_HEADS), 0.05),
        'ssm_norm': 1.0 + nrm((no, SSM_INNER), 0.05),
        'ssm_w_out': nrm((no, SSM_INNER, D), SSM_INNER ** -0.5),
        'router_w': nrm((DEPTH, D, N_EXPERTS), D ** -0.5),
        'router_b': nrm((DEPTH, N_EXPERTS), 0.01),
        'moe_w_gu': nrm((DEPTH, N_EXPERTS, D, 2 * D_EXPERT), D ** -0.5),
        'moe_b_gu': nrm((DEPTH, N_EXPERTS, 2 * D_EXPERT), 0.02),
        'moe_w_down': nrm((DEPTH, N_EXPERTS, D_EXPERT, D), D_EXPERT ** -0.5),
        'moe_b_down': nrm((DEPTH, N_EXPERTS, D), 0.02),
    }


def reference(x, c, ada_w, ada_b, norm_mix, norm_ffn, attn_w_in, attn_w_out, nsa_q_norm, nsa_k_norm,
              cmp_pe_k, cmp_pe_v, cmp_w1_k, cmp_w2_k, cmp_w1_v, cmp_w2_v, ssm_w_in, ssm_conv_w, ssm_conv_b,
              ssm_dt_bias, ssm_a_log, ssm_d, ssm_norm, ssm_w_out, router_w, router_b, moe_w_gu, moe_b_gu,
              moe_w_down, moe_b_down):
    h = x
    cond = jax.nn.silu(c)
    for layer in range(DEPTH):
        mod = (cond @ ada_w[layer] + ada_b[layer])[:, None, :]
        sh1, sc1, g1, sh2, sc2, g2 = jnp.split(mod, 6, axis=-1)
        u = rms_norm(h, norm_mix[layer]) * (1.0 + sc1) + sh1
        i = layer // 2
        if layer % 2 == 0:
            mix = attention_mixer(u, attn_w_in[i], attn_w_out[i], nsa_q_norm[i], nsa_k_norm[i], cmp_pe_k[i],
                                  cmp_pe_v[i], cmp_w1_k[i], cmp_w2_k[i], cmp_w1_v[i], cmp_w2_v[i])
        else:
            mix = mamba2_mixer(u, ssm_w_in[i], ssm_conv_w[i], ssm_conv_b[i], ssm_dt_bias[i], ssm_a_log[i],
                               ssm_d[i], ssm_norm[i], ssm_w_out[i])
        h = h + g1 * mix
        u = rms_norm(h, norm_ffn[layer]) * (1.0 + sc2) + sh2
        h = h + g2 * moe_ffn(u, router_w[layer], router_b[layer], moe_w_gu[layer], moe_b_gu[layer],
                             moe_w_down[layer], moe_b_down[layer])
    return h
```

```python
import functools
import math

import numpy as np
import jax
import jax.numpy as jnp
from jax import lax
from jax.experimental import pallas as pl
from jax.experimental.pallas import tpu as pltpu

F32 = jnp.float32
BF16 = jnp.bfloat16
I32 = jnp.int32
HIGHEST = lax.Precision.HIGHEST

D_MODEL = 1024
HEAD_DIM = 64
LANES = 128
SB_HEADS = 8
NSA_HEADS = 8
NSA_KV_GROUPS = 2
NSA_REP = NSA_HEADS // NSA_KV_GROUPS
CMP_BLOCK = 32
CMP_STRIDE = 16
CMP_HIDDEN = 128
SLC_BLOCK = 64
SLC_TOPN = 16
WINDOW = 512
FORCE_SCORE = 1e6
SB_W = SB_HEADS * HEAD_DIM
NSA_QW = NSA_HEADS * HEAD_DIM
SSM_INNER = 2 * D_MODEL
SSM_HEAD_DIM = 64
SSM_HEADS = SSM_INNER // SSM_HEAD_DIM
SSM_GROUPS = 4
SSM_GROUP_HEADS = SSM_HEADS // SSM_GROUPS
SSM_GROUP_W = SSM_INNER // SSM_GROUPS
SSM_STATE = 128
SSM_CONV = 4
SSM_CHUNK = 256
SSM_CONV_DIM = SSM_INNER + 2 * SSM_GROUPS * SSM_STATE
N_EXPERTS = 32
TOP_K = 4
D_EXPERT = D_MODEL
SWIGLU_LIMIT = 7.0
SWIGLU_ALPHA = 1.702
EPS = 1e-6
NEG_BIG = -1e30

VMEM_LIMIT = 56 * 1024 * 1024

ROW_TILE = 256
SB_TILE = 256
NSA_TQ = 128
NSA_TK = 256
MOE_TILE = 256
CONV_ROWS = 512
CONV_COLS = 1024


def _cparams(sem):
    return pltpu.CompilerParams(dimension_semantics=sem, vmem_limit_bytes=VMEM_LIMIT)


def _split_dot(a, m01, terms):
    out = None
    r = a
    for t in range(terms):
        hi = r.astype(BF16)
        d = jnp.dot(hi, m01, preferred_element_type=F32)
        out = d if out is None else out + d
        if t + 1 < terms:
            r = r - hi.astype(F32)
    return out


def _split_dot_left(m01, a, terms):
    out = None
    r = a
    for t in range(terms):
        hi = r.astype(BF16)
        d = jnp.dot(m01, hi, preferred_element_type=F32)
        out = d if out is None else out + d
        if t + 1 < terms:
            r = r - hi.astype(F32)
    return out


def _dot_nt(a, b):
    return lax.dot_general(a, b, (((1,), (1,)), ((), ())), preferred_element_type=F32)


def _softplus(x):
    return jnp.maximum(x, 0.0) + jnp.log(1.0 + jnp.exp(-jnp.abs(x)))


def _modulated_norm(h, gain, sc, sh):
    ms = jnp.mean(h * h, axis=-1, keepdims=True)
    y = h * lax.rsqrt(ms + EPS) * gain
    return y * (1.0 + sc) + sh


def _seg_rmsnorm(x, bd, gain):
    ssq = _split_dot(x * x, bd, 2)
    return x * lax.rsqrt(ssq * (1.0 / HEAD_DIM) + EPS) * gain


def _mod_specs(layer, whichs, tiles_per_batch):
    def mk(which):
        return pl.BlockSpec((None, None, None, 1, D_MODEL),
                            lambda i, *_: (layer, i // tiles_per_batch, which, 0, 0))
    return [mk(w) for w in whichs]


def _adaln_kernel(c_ref, w_ref, b_ref, o_ref):
    c = c_ref[...]
    s = c * jax.nn.sigmoid(c)
    o_ref[...] = jnp.dot(s, w_ref[...], precision=HIGHEST, preferred_element_type=F32) + b_ref[...]


def _adaln(c_pad, ada_w, ada_b):
    depth, d, n = ada_w.shape
    bp = c_pad.shape[0]
    tn = 1536
    return pl.pallas_call(
        _adaln_kernel,
        grid=(depth, n // tn),
        in_specs=[pl.BlockSpec((bp, d), lambda l, j: (0, 0)),
                  pl.BlockSpec((None, d, tn), lambda l, j: (l, 0, j)),
                  pl.BlockSpec((None, 1, tn), lambda l, j: (l, 0, j))],
        out_specs=pl.BlockSpec((None, bp, tn), lambda l, j: (l, 0, j)),
        out_shape=jax.ShapeDtypeStruct((depth, bp, n), F32),
        compiler_params=_cparams(("arbitrary", "arbitrary")),
        name="adaln",
    )(c_pad, ada_w, ada_b.reshape(depth, 1, n))


ATTN_SB0, ATTN_NQ0, ATTN_KVC0, ATTN_NKV0, ATTN_GT0, ATTN_COLS = 0, 1536, 2048, 2304, 3328, 3584


def _attn_weight_layout(w_in):
    d = w_in.shape[0]
    off = 3 * SB_W + NSA_QW
    gw = NSA_KV_GROUPS * HEAD_DIM
    parts = [w_in[:, :off + 2 * gw]]
    for kind in range(4):
        base = off + 2 * gw + kind * gw
        for g in range(NSA_KV_GROUPS):
            blk = w_in[:, base + g * HEAD_DIM: base + (g + 1) * HEAD_DIM]
            parts += [blk, blk]
    gbase = off + 6 * gw
    for g in range(NSA_KV_GROUPS):
        blk = w_in[:, gbase + g * NSA_REP * 3: gbase + (g + 1) * NSA_REP * 3]
        parts.append(jnp.pad(blk, ((0, 0), (0, LANES - NSA_REP * 3))))
    w = jnp.concatenate(parts, axis=1)
    assert w.shape == (d, ATTN_COLS)
    return w.astype(BF16)


def _attn_inproj_kernel(h_ref, gain_ref, sc_ref, sh_ref, w_ref, qg_ref, k1g_ref, k2g_ref, bd_ref,
                        sb_ref, nq_ref, kvc_ref, nkv_ref, gt_ref):
    u = _modulated_norm(h_ref[...], gain_ref[...], sc_ref[...], sh_ref[...]).astype(BF16)
    bd = bd_ref[...]
    scale = HEAD_DIM ** -0.5

    def proj(a, b):
        return jnp.dot(u, w_ref[:, a:b], preferred_element_type=F32)

    sb_ref[:, 0:SB_W] = (proj(0, SB_W) * scale).astype(BF16)
    sb_ref[:, SB_W:3 * SB_W] = proj(SB_W, 3 * SB_W).astype(BF16)
    for j in range(NSA_QW // LANES):
        x = proj(ATTN_NQ0 + LANES * j, ATTN_NQ0 + LANES * (j + 1))
        nq_ref[:, LANES * j:LANES * (j + 1)] = (_seg_rmsnorm(x, bd, qg_ref[...]) * scale).astype(BF16)
    kvc_ref[...] = proj(ATTN_KVC0, ATTN_NKV0).astype(BF16)
    for kind in range(4):
        for g in range(NSA_KV_GROUPS):
            c = (kind * NSA_KV_GROUPS + g) * LANES
            x = proj(ATTN_NKV0 + c, ATTN_NKV0 + c + LANES)
            if kind == 0:
                x = _seg_rmsnorm(x, bd, k1g_ref[...])
            elif kind == 2:
                x = _seg_rmsnorm(x, bd, k2g_ref[...])
            nkv_ref[:, c:c + LANES] = x.astype(BF16)
    gt_ref[...] = jax.nn.sigmoid(proj(ATTN_GT0, ATTN_COLS))


def _attn_inproj(h2, mod, layer, gain, w_bf, qg, k1g, k2g, bd, seq):
    t, d = h2.shape
    tm = ROW_TILE
    tpb = seq // tm
    row = lambda n: pl.BlockSpec((tm, n), lambda i: (i, 0))
    const = lambda shp: pl.BlockSpec(shp, lambda i: (0,) * len(shp))
    return pl.pallas_call(
        _attn_inproj_kernel,
        grid=(t // tm,),
        in_specs=[row(d), const((1, d))] + _mod_specs(layer, (1, 0), tpb)
                 + [const((d, ATTN_COLS)), const((1, LANES)), const((1, LANES)), const((1, LANES)),
                    const((LANES, LANES))],
        out_specs=[row(3 * SB_W), row(NSA_QW), row(2 * LANES), row(8 * LANES), row(2 * LANES)],
        out_shape=[jax.ShapeDtypeStruct((t, 3 * SB_W), BF16), jax.ShapeDtypeStruct((t, NSA_QW), BF16),
                   jax.ShapeDtypeStruct((t, 2 * LANES), BF16), jax.ShapeDtypeStruct((t, 8 * LANES), BF16),
                   jax.ShapeDtypeStruct((t, 2 * LANES), F32)],
        compiler_params=_cparams(("arbitrary",)),
        name="attn_inproj",
    )(h2, gain, mod, mod, w_bf, qg, k1g, k2g, bd)


def _sb_kernel(q_ref, k_ref, v_ref, u_ref, o_ref, *, tile):
    qi = pl.program_id(2)
    lane = lax.broadcasted_iota(I32, (1, LANES), 1)
    low = lane < HEAD_DIM
    q = q_ref[...]
    zero = jnp.zeros_like(q)
    qs = (jnp.where(low, q, zero), jnp.where(low, zero, q))
    u = u_ref[...]
    row = lax.broadcasted_iota(I32, (tile, tile), 0)
    col = lax.broadcasted_iota(I32, (tile, tile), 1)
    causal = col < row

    def block(kb, accs, runs, masked):
        off = pl.multiple_of(kb * tile, tile)
        k = k_ref[pl.ds(off, tile), :]
        v = v_ref[pl.ds(off, tile), :]
        new_accs, new_runs = [], []
        for hh in range(2):
            x = _dot_nt(qs[hh], k)
            sp = _softplus(x)
            lk = -sp
            if masked:
                lk = jnp.where(causal, lk, 0.0)
            tot = _split_dot(lk, u, 2) + runs[hh]
            w = jnp.exp(x - sp + tot)
            if masked:
                w = jnp.where(causal, w, 0.0)
            pv = jnp.dot(w.astype(BF16), v, preferred_element_type=F32)
            new_accs.append(accs[hh] + pv)
            new_runs.append(runs[hh] + jnp.sum(lk, axis=1, keepdims=True))
        return tuple(new_accs), tuple(new_runs)

    acc0 = (jnp.zeros((tile, LANES), F32),) * 2
    run0 = (jnp.zeros((tile, 1), F32),) * 2
    accs, runs = block(qi, acc0, run0, True)

    def body(i, carry):
        return block(qi - 1 - i, carry[0], carry[1], False)

    accs, runs = lax.fori_loop(0, qi, body, (accs, runs))
    o_ref[...] = jnp.where(low, accs[0], accs[1]).astype(BF16)


def _sb_attention(sb, batch, seq):
    tile = SB_TILE
    nq = seq // tile
    pairs = SB_W // LANES
    r = np.arange(tile)
    u = jnp.asarray(r[:, None] > r[None, :], BF16)
    return pl.pallas_call(
        functools.partial(_sb_kernel, tile=tile),
        grid=(batch, pairs, nq),
        in_specs=[pl.BlockSpec((tile, LANES), lambda b, p, i: (b * nq + i, p)),
                  pl.BlockSpec((seq, LANES), lambda b, p, i: (b, pairs + p)),
                  pl.BlockSpec((seq, LANES), lambda b, p, i: (b, 2 * pairs + p)),
                  pl.BlockSpec((tile, tile), lambda b, p, i: (0, 0))],
        out_specs=pl.BlockSpec((tile, LANES), lambda b, p, i: (b * nq + i, p)),
        out_shape=jax.ShapeDtypeStruct((batch * seq, SB_W), BF16),
        compiler_params=_cparams(("arbitrary",) * 3),
        name="sb_attention",
    )(sb, sb, sb, u)


def _compress_kernel(hb_ref, wc_ref, pe_ref, w1k_ref, w1v_ref, w2k_ref, w2v_ref, kg_ref, bd_ref,
                     kc_ref, vc_ref):
    p = jnp.dot(hb_ref[...], wc_ref[...], preferred_element_type=F32)
    n_half = p.shape[0]
    for kind in range(2):
        w1 = (w1k_ref, w1v_ref)[kind][...]
        w2 = (w2k_ref, w2v_ref)[kind][...]
        pe_term = jnp.dot(pe_ref[kind], w1, precision=HIGHEST, preferred_element_type=F32)[0:1, :]
        for g in range(NSA_KV_GROUPS):
            c = (kind * NSA_KV_GROUPS + g) * 2 * LANES
            a = p[:, c:c + LANES]
            b = p[:, c + LANES:c + 2 * LANES]
            pre = a + pltpu.roll(b, n_half - 1, 0) + pe_term
            hid = pre * jax.nn.sigmoid(pre)
            out = jnp.dot(hid.astype(BF16), w2, preferred_element_type=F32)
            if kind == 0:
                kc_ref[g] = _seg_rmsnorm(out, bd_ref[...], kg_ref[...]).astype(BF16)
            else:
                vc_ref[g] = out.astype(BF16)


def _compress_weights(w1k, w1v):
    cols = []
    for kind, w1 in enumerate((w1k, w1v)):
        w1r = w1.reshape(2, CMP_STRIDE, HEAD_DIM, CMP_HIDDEN)
        for g in range(NSA_KV_GROUPS):
            seg = kind * NSA_KV_GROUPS + g
            for half in range(2):
                full = jnp.zeros((CMP_STRIDE, 2 * NSA_KV_GROUPS, HEAD_DIM, CMP_HIDDEN), F32)
                full = full.at[:, seg].set(w1r[half])
                cols.append(full.reshape(CMP_STRIDE * 2 * NSA_KV_GROUPS * HEAD_DIM, CMP_HIDDEN))
    return jnp.concatenate(cols, axis=1).astype(BF16)


def _compress(kvc, batch, seq, pe_k, pe_v, w1k, w1v, w2k, w2v, k0g, bd):
    n_half = seq // CMP_STRIDE
    width = CMP_STRIDE * 2 * LANES
    hb = kvc.reshape(batch, n_half, width)
    wc = _compress_weights(w1k, w1v)
    fk = CMP_BLOCK * HEAD_DIM
    pe = jnp.stack([jnp.broadcast_to(pe_k.reshape(1, fk), (8, fk)),
                    jnp.broadcast_to(pe_v.reshape(1, fk), (8, fk))])
    dup = lambda w: jnp.concatenate([w, w], axis=1).astype(BF16)
    const = lambda shp: pl.BlockSpec(shp, lambda b: (0,) * len(shp))
    out_sds = jax.ShapeDtypeStruct((batch, NSA_KV_GROUPS, n_half, LANES), BF16)
    out_spec = pl.BlockSpec((None, NSA_KV_GROUPS, n_half, LANES), lambda b: (b, 0, 0, 0))
    return pl.pallas_call(
        _compress_kernel,
        grid=(batch,),
        in_specs=[pl.BlockSpec((None, n_half, width), lambda b: (b, 0, 0)),
                  const(wc.shape), const(pe.shape), const(w1k.shape), const(w1v.shape),
                  const((CMP_HIDDEN, LANES)), const((CMP_HIDDEN, LANES)), const((1, LANES)),
                  const((LANES, LANES))],
        out_specs=[out_spec, out_spec],
        out_shape=[out_sds, out_sds],
        compiler_params=_cparams(("arbitrary",)),
        name="nsa_compress",
    )(hb, wc, pe, w1k, w1v, dup(w2k), dup(w2v), k0g, bd)


def _alibi_slopes(n):
    return [float(v) for v in np.asarray(2.0 ** (-8.0 * np.arange(1, n + 1) / n), np.float32)]


def _nsa_kernel(q_ref, kc_ref, vc_ref, ks_ref, vs_ref, kw_ref, vw_ref, gt_ref, ov_ref, e_ref, o_ref,
                *, tq, tk, seq):
    g = pl.program_id(1)
    qi = pl.program_id(2)
    t0 = qi * tq
    lane = lax.broadcasted_iota(I32, (1, LANES), 1)
    low = lane < HEAD_DIM
    tpos = t0 + lax.broadcasted_iota(I32, (tq, 1), 0)
    slopes = _alibi_slopes(NSA_HEADS)
    slope = [jnp.where(g == 0, slopes[r], slopes[NSA_REP + r]) for r in range(NSA_REP)]

    q = q_ref[...]
    qz = []
    for r in range(NSA_REP):
        blk = q[:, LANES * (r // 2):LANES * (r // 2 + 1)]
        zero = jnp.zeros_like(blk)
        qz.append(jnp.where(low, blk, zero) if r % 2 == 0 else jnp.where(low, zero, blk))

    kc = kc_ref[...]
    vc = vc_ref[...]
    n_cmp = kc.shape[0]
    cend = lax.broadcasted_iota(I32, (1, n_cmp), 1) * CMP_STRIDE + (CMP_BLOCK - 1)
    mask_c = cend <= tpos
    dist_c = (tpos - cend).astype(F32)
    o_c = []
    psum = jnp.zeros((tq, n_cmp), F32)
    for r in range(NSA_REP):
        s = _dot_nt(qz[r], kc) - slope[r] * dist_c
        s = jnp.where(mask_c, s, -jnp.inf)
        m = jnp.max(s, axis=1, keepdims=True)
        m = jnp.where(m == -jnp.inf, 0.0, m)
        p = jnp.exp(s - m)
        p = p / jnp.maximum(jnp.sum(p, axis=1, keepdims=True), 1e-30)
        o_c.append(jnp.dot(p.astype(BF16), vc, preferred_element_type=F32))
        psum = psum + p

    n_slc = seq // SLC_BLOCK
    p_slc = _split_dot(psum, ov_ref[...], 2)
    cur = jnp.right_shift(tpos, int(math.log2(SLC_BLOCK)))
    valid = lane * SLC_BLOCK <= tpos
    forced = (lane == 0) | (lane == cur) | (lane == cur - 1)
    score = jnp.where(valid, jnp.where(forced, FORCE_SCORE, p_slc), -FORCE_SCORE)
    sc_t = score.T[0:n_slc, :]
    jrow = lax.broadcasted_iota(I32, (n_slc, 1), 0)
    cnt = jnp.zeros((n_slc, tq), F32)
    for i in range(n_slc):
        ri = sc_t[i:i + 1, :]
        cnt = cnt + jnp.where(jrow > i, jnp.where(ri >= sc_t, 1.0, 0.0), jnp.where(ri > sc_t, 1.0, 0.0))
    sel_t = jnp.where(cnt < float(min(SLC_TOPN, n_slc)), 1.0, 0.0)
    if n_slc < LANES:
        sel_t = jnp.concatenate([sel_t, jnp.zeros((LANES - n_slc, tq), F32)], axis=0)
    sel = sel_t.T.astype(BF16)

    def sel_body(kt, carry):
        ms, ls, accs = carry
        off = pl.multiple_of(kt * tk, tk)
        k = ks_ref[pl.ds(off, tk), :]
        v = vs_ref[pl.ds(off, tk), :]
        member = jnp.dot(sel, e_ref[:, pl.ds(off, tk)], preferred_element_type=F32)
        pos = off + lax.broadcasted_iota(I32, (1, tk), 1)
        ok = jnp.where(pos <= tpos, member, 0.0) > 0.5
        dist = (tpos - pos).astype(F32)
        new_ms, new_ls, new_accs = [], [], []
        for r in range(NSA_REP):
            s = _dot_nt(qz[r], k) - slope[r] * dist
            s = jnp.where(ok, s, NEG_BIG)
            m_new = jnp.maximum(ms[r], jnp.max(s, axis=1, keepdims=True))
            alpha = jnp.exp(ms[r] - m_new)
            p = jnp.exp(s - m_new)
            new_ls.append(alpha * ls[r] + jnp.sum(p, axis=1, keepdims=True))
            new_accs.append(alpha * accs[r] + jnp.dot(p.astype(BF16), v, preferred_element_type=F32))
            new_ms.append(m_new)
        return tuple(new_ms), tuple(new_ls), tuple(new_accs)

    n_kt = (t0 + tq + tk - 1) // tk
    init = ((jnp.full((tq, 1), NEG_BIG, F32),) * NSA_REP, (jnp.zeros((tq, 1), F32),) * NSA_REP,
            (jnp.zeros((tq, LANES), F32),) * NSA_REP)
    ms, ls, accs = lax.fori_loop(0, n_kt, sel_body, init)
    o_s = [accs[r] / ls[r] for r in range(NSA_REP)]

    span = WINDOW + tq
    start = pl.multiple_of(jnp.maximum(t0 - WINDOW, 0), tq)
    kw = kw_ref[pl.ds(start, span), :]
    vw = vw_ref[pl.ds(start, span), :]
    pos = start + lax.broadcasted_iota(I32, (1, span), 1)
    ok_w = jnp.where(pos <= tpos, pos, -WINDOW - 1) > tpos - WINDOW
    dist_w = (tpos - pos).astype(F32)
    o_w = []
    for r in range(NSA_REP):
        s = _dot_nt(qz[r], kw) - slope[r] * dist_w
        s = jnp.where(ok_w, s, -jnp.inf)
        m = jnp.max(s, axis=1, keepdims=True)
        p = jnp.exp(s - m)
        p = p / jnp.maximum(jnp.sum(p, axis=1, keepdims=True), 1e-30)
        o_w.append(jnp.dot(p.astype(BF16), vw, preferred_element_type=F32))

    gt = gt_ref[...]
    outs = []
    for r in range(NSA_REP):
        outs.append(gt[:, 3 * r:3 * r + 1] * o_c[r] + gt[:, 3 * r + 1:3 * r + 2] * o_s[r]
                    + gt[:, 3 * r + 2:3 * r + 3] * o_w[r])
    for j in range(NSA_REP // 2):
        o_ref[:, LANES * j:LANES * (j + 1)] = jnp.where(low, outs[2 * j], outs[2 * j + 1]).astype(BF16)


def _nsa_attention(nq, kcmp, vcmp, nkv, gates, batch, seq):
    tq, tk = NSA_TQ, NSA_TK
    nqt = seq // tq
    n_cmp_rows = kcmp.shape[2]
    n_slc = seq // SLC_BLOCK
    cmp_start = np.arange(n_cmp_rows) * CMP_STRIDE
    slc_start = np.arange(LANES) * SLC_BLOCK
    overlap = ((cmp_start[:, None] < slc_start[None, :] + SLC_BLOCK)
               & (cmp_start[:, None] + CMP_BLOCK > slc_start[None, :])
               & (np.arange(LANES)[None, :] < n_slc))
    ov = jnp.asarray(overlap, BF16)
    expand = jnp.asarray(np.arange(LANES)[:, None] == (np.arange(seq)[None, :] // SLC_BLOCK), BF16)
    gw = NSA_REP * HEAD_DIM
    kv_spec = lambda kind: pl.BlockSpec((seq, LANES), lambda b, g, i: (b, kind * NSA_KV_GROUPS + g))
    cmp_spec = pl.BlockSpec((None, None, n_cmp_rows, LANES), lambda b, g, i: (b, g, 0, 0))
    return pl.pallas_call(
        functools.partial(_nsa_kernel, tq=tq, tk=tk, seq=seq),
        grid=(batch, NSA_KV_GROUPS, nqt),
        in_specs=[pl.BlockSpec((tq, gw), lambda b, g, i: (b * nqt + i, g)),
                  cmp_spec, cmp_spec, kv_spec(0), kv_spec(1), kv_spec(2), kv_spec(3),
                  pl.BlockSpec((tq, LANES), lambda b, g, i: (b * nqt + i, g)),
                  pl.BlockSpec(ov.shape, lambda b, g, i: (0, 0)),
                  pl.BlockSpec(expand.shape, lambda b, g, i: (0, 0))],
        out_specs=pl.BlockSpec((tq, gw), lambda b, g, i: (b * nqt + i, g)),
        out_shape=jax.ShapeDtypeStruct((batch * seq, NSA_QW), BF16),
        compiler_params=_cparams(("arbitrary",) * 3),
        name="nsa_attention",
    )(nq, kcmp, vcmp, nkv, nkv, nkv, nkv, gates, ov, expand)


def _proj_residual_kernel(*refs, n_in):
    x_refs, w_refs = refs[:n_in], refs[n_in:2 * n_in]
    h_ref, g_ref, o_ref = refs[2 * n_in:]
    acc = None
    for x_ref, w_ref in zip(x_refs, w_refs):
        d = jnp.dot(x_ref[...], w_ref[...], preferred_element_type=F32)
        acc = d if acc is None else acc + d
    o_ref[...] = h_ref[...] + g_ref[...] * acc


def _proj_residual(xs, ws, h2, mod, layer, which, seq):
    t, d = h2.shape
    tm = ROW_TILE
    n_in = len(xs)
    row = lambda n: pl.BlockSpec((tm, n), lambda i: (i, 0))
    const = lambda shp: pl.BlockSpec(shp, lambda i: (0,) * len(shp))
    return pl.pallas_call(
        functools.partial(_proj_residual_kernel, n_in=n_in),
        grid=(t // tm,),
        in_specs=[row(x.shape[1]) for x in xs] + [const(w.shape) for w in ws] + [row(d)]
                 + _mod_specs(layer, (which,), seq // tm),
        out_specs=row(d),
        out_shape=jax.ShapeDtypeStruct((t, d), F32),
        compiler_params=_cparams(("arbitrary",)),
        name="proj_residual",
    )(*xs, *ws, h2, mod)


def _router_kernel(h_ref, gain_ref, sc_ref, sh_ref, w_ref, b_ref, tri_ref,
                   u_ref, e_ref, p_ref, r_ref, c_ref, cnt_ref):
    @pl.when(pl.program_id(0) == 0)
    def _():
        cnt_ref[...] = jnp.zeros_like(cnt_ref)

    u = _modulated_norm(h_ref[...], gain_ref[...], sc_ref[...], sh_ref[...])
    u_ref[...] = u.astype(BF16)
    logits = jnp.dot(u, w_ref[...], precision=HIGHEST, preferred_element_type=F32) + b_ref[...]
    tm = logits.shape[0]
    lane = lax.broadcasted_iota(I32, (tm, LANES), 1)
    lane_f = lane.astype(F32)
    work = logits
    tops, idxs, hots = [], [], []
    for _ in range(TOP_K):
        m = jnp.max(work, axis=1, keepdims=True)
        idx = jnp.min(jnp.where(work == m, lane_f, float(LANES)), axis=1, keepdims=True)
        hot = lane_f == idx
        work = jnp.where(hot, -jnp.inf, work)
        tops.append(m)
        idxs.append(idx)
        hots.append(hot)
    ex = [jnp.exp(m - tops[0]) for m in tops]
    den = ex[0] + ex[1] + ex[2] + ex[3]
    any_hot = jnp.zeros((tm, LANES), F32)
    for hot in hots:
        any_hot = any_hot + jnp.where(hot, 1.0, 0.0)
    before = jnp.dot(tri_ref[...], any_hot.astype(BF16), preferred_element_type=F32) + cnt_ref[...]
    e_out = jnp.zeros((tm, LANES), F32)
    p_out = jnp.zeros((tm, LANES), F32)
    r_out = jnp.zeros((tm, LANES), F32)
    for k in range(TOP_K):
        rank = jnp.sum(jnp.where(hots[k], before, 0.0), axis=1, keepdims=True)
        e_out = jnp.where(lane == k, idxs[k], e_out)
        p_out = jnp.where(lane == k, ex[k] / den, p_out)
        r_out = jnp.where(lane == k, rank, r_out)
    e_ref[...] = e_out.astype(I32)
    p_ref[...] = p_out
    r_ref[...] = r_out.astype(I32)
    cnt = cnt_ref[...] + jnp.sum(any_hot, axis=0, keepdims=True)
    cnt_ref[...] = cnt
    c_ref[...] = jnp.broadcast_to(cnt, c_ref.shape).astype(I32)


def _router(h2, mod, layer, gain, w_router, b_router, seq):
    t, d = h2.shape
    tm = ROW_TILE
    wp = jnp.pad(w_router, ((0, 0), (0, LANES - N_EXPERTS)))
    bp = jnp.pad(b_router, (0, LANES - N_EXPERTS), constant_values=NEG_BIG).reshape(1, LANES)
    r = np.arange(tm)
    tri = jnp.asarray(r[:, None] > r[None, :], BF16)
    row = lambda n: pl.BlockSpec((tm, n), lambda i: (i, 0))
    const = lambda shp: pl.BlockSpec(shp, lambda i: (0,) * len(shp))
    return pl.pallas_call(
        _router_kernel,
        grid=(t // tm,),
        in_specs=[row(d), const((1, d))] + _mod_specs(layer, (4, 3), seq // tm)
                 + [const((d, LANES)), const((1, LANES)), const((tm, tm))],
        out_specs=[row(d), row(LANES), row(LANES), row(LANES), const((8, LANES))],
        out_shape=[jax.ShapeDtypeStruct((t, d), BF16), jax.ShapeDtypeStruct((t, LANES), I32),
                   jax.ShapeDtypeStruct((t, LANES), F32), jax.ShapeDtypeStruct((t, LANES), I32),
                   jax.ShapeDtypeStruct((8, LANES), I32)],
        scratch_shapes=[pltpu.VMEM((1, LANES), F32)],
        compiler_params=_cparams(("arbitrary",)),
        name="moe_router",
    )(h2, gain, mod, mod, wp, bp, tri)


def _expert_kernel(be_ref, nu_ref, x_ref, wgu_ref, bgu_ref, wd_ref, bd_ref, o_ref, wgu_bf, wd_bf):
    i = pl.program_id(0)
    prev = be_ref[jnp.maximum(i - 1, 0)]
    fresh = (i == 0) | (be_ref[i] != prev)

    @pl.when(fresh & (i < nu_ref[0]))
    def _():
        rows = 128

        def cast_gu(j, c):
            off = pl.multiple_of(j * rows, rows)
            wgu_bf[pl.ds(off, rows), :] = wgu_ref[pl.ds(off, rows), :].astype(BF16)
            wd_bf[pl.ds(off, rows), :] = wd_ref[pl.ds(off, rows), :].astype(BF16)
            return c

        lax.fori_loop(0, D_MODEL // rows, cast_gu, 0)

    @pl.when(i < nu_ref[0])
    def _():
        gu = jnp.dot(x_ref[...], wgu_bf[...], preferred_element_type=F32) + bgu_ref[...]
        gate = jnp.minimum(gu[:, :D_EXPERT], SWIGLU_LIMIT)
        up = jnp.clip(gu[:, D_EXPERT:], -SWIGLU_LIMIT, SWIGLU_LIMIT)
        act = (up + 1.0) * gate * jax.nn.sigmoid(SWIGLU_ALPHA * gate)
        o_ref[...] = jnp.dot(act.astype(BF16), wd_bf[...], preferred_element_type=F32) + bd_ref[...]

    @pl.when(i >= nu_ref[0])
    def _():
        o_ref[...] = jnp.zeros_like(o_ref)


def _experts(xs, blk_e, n_used, w_gu, b_gu, w_down, b_down):
    n_slots, d = xs.shape
    tm = MOE_TILE
    ne, _, dgu = w_gu.shape
    grid_spec = pltpu.PrefetchScalarGridSpec(
        num_scalar_prefetch=2,
        grid=(n_slots // tm,),
        in_specs=[pl.BlockSpec((tm, d), lambda i, be, nu: (i, 0)),
                  pl.BlockSpec((None, d, dgu), lambda i, be, nu: (be[i], 0, 0)),
                  pl.BlockSpec((None, 1, dgu), lambda i, be, nu: (be[i], 0, 0)),
                  pl.BlockSpec((None, D_EXPERT, d), lambda i, be, nu: (be[i], 0, 0)),
                  pl.BlockSpec((None, 1, d), lambda i, be, nu: (be[i], 0, 0))],
        out_specs=pl.BlockSpec((tm, d), lambda i, be, nu: (i, 0)),
        scratch_shapes=[pltpu.VMEM((d, dgu), BF16), pltpu.VMEM((D_EXPERT, d), BF16)],
    )
    return pl.pallas_call(
        _expert_kernel,
        grid_spec=grid_spec,
        out_shape=jax.ShapeDtypeStruct((n_slots, d), F32),
        compiler_params=_cparams(("arbitrary",)),
        name="moe_experts",
    )(blk_e, n_used, xs, w_gu, b_gu.reshape(ne, 1, dgu), w_down, b_down.reshape(ne, 1, d))


def _combine_kernel(y_ref, p_ref, h_ref, g_ref, o_ref):
    p = p_ref[...]
    acc = p[:, 0:1] * y_ref[0]
    for k in range(1, TOP_K):
        acc = acc + p[:, k:k + 1] * y_ref[k]
    o_ref[...] = h_ref[...] + g_ref[...] * acc


def _combine(yg, top_p, h2, mod, layer, seq):
    t, d = h2.shape
    tm = ROW_TILE
    row = lambda n: pl.BlockSpec((tm, n), lambda i: (i, 0))
    return pl.pallas_call(
        _combine_kernel,
        grid=(t // tm,),
        in_specs=[pl.BlockSpec((TOP_K, tm, d), lambda i: (0, i, 0)), row(LANES), row(d)]
                 + _mod_specs(layer, (5,), seq // tm),
        out_specs=row(d),
        out_shape=jax.ShapeDtypeStruct((t, d), F32),
        compiler_params=_cparams(("arbitrary",)),
        name="moe_combine",
    )(yg, top_p, h2, mod)


def _moe(h2, mod, layer, gain, w_router, b_router, w_gu, b_gu, w_down, b_down, seq):
    t, d = h2.shape
    tm = MOE_TILE
    u, top_e, top_p, rank, counts = _router(h2, mod, layer, gain, w_router, b_router, seq)
    counts = counts[0, :N_EXPERTS]
    padded = (counts + tm - 1) // tm * tm
    pad_end = jnp.cumsum(padded)
    pad_start = pad_end - padded
    slot = pad_start[top_e[:, :TOP_K]] + rank[:, :TOP_K]
    n_slots = t * TOP_K + N_EXPERTS * tm
    tok = jnp.broadcast_to(jnp.arange(t, dtype=I32)[:, None], (t, TOP_K))
    slot_tok = jnp.zeros((n_slots,), I32).at[slot.reshape(-1)].set(tok.reshape(-1))
    n_blocks = n_slots // tm
    blk_e = jnp.minimum(jnp.searchsorted(pad_end, jnp.arange(n_blocks, dtype=I32) * tm, side='right'),
                        N_EXPERTS - 1).astype(I32)
    n_used = (pad_end[-1] // tm).astype(I32).reshape(1)
    xs = jnp.take(u, slot_tok, axis=0)
    ys = _experts(xs, blk_e, n_used, w_gu, b_gu, w_down, b_down)
    yg = jnp.take(ys, slot.T, axis=0)
    return _combine(yg, top_p, h2, mod, layer, seq)


SSM_Z0, SSM_X0, SSM_DT0, SSM_COLS = 0, SSM_INNER, SSM_INNER + SSM_CONV_DIM, SSM_INNER + SSM_CONV_DIM + SSM_GROUPS * LANES


def _ssm_weight_layout(w_in):
    parts = [w_in[:, :SSM_DT0]]
    for g in range(SSM_GROUPS):
        blk = w_in[:, SSM_DT0 + g * SSM_GROUP_HEADS: SSM_DT0 + (g + 1) * SSM_GROUP_HEADS]
        parts.append(jnp.pad(blk, ((0, 0), (0, LANES - SSM_GROUP_HEADS))))
    return jnp.concatenate(parts, axis=1).astype(BF16)


def _ssm_inproj_kernel(h_ref, gain_ref, sc_ref, sh_ref, w_ref, z_ref, xbc_ref, dt_ref):
    u = _modulated_norm(h_ref[...], gain_ref[...], sc_ref[...], sh_ref[...]).astype(BF16)
    step = 512
    for c in range(0, SSM_X0, step):
        z_ref[:, c:c + step] = jnp.dot(u, w_ref[:, c:c + step], preferred_element_type=F32)
    for c in range(0, SSM_CONV_DIM, step):
        xbc_ref[:, c:c + step] = jnp.dot(u, w_ref[:, SSM_X0 + c:SSM_X0 + c + step], preferred_element_type=F32)
    dt_ref[...] = jnp.dot(u, w_ref[:, SSM_DT0:SSM_COLS], preferred_element_type=F32)


def _ssm_inproj(h2, mod, layer, gain, w_bf, seq):
    t, d = h2.shape
    tm = ROW_TILE
    row = lambda n: pl.BlockSpec((tm, n), lambda i: (i, 0))
    const = lambda shp: pl.BlockSpec(shp, lambda i: (0,) * len(shp))
    return pl.pallas_call(
        _ssm_inproj_kernel,
        grid=(t // tm,),
        in_specs=[row(d), const((1, d))] + _mod_specs(layer, (1, 0), seq // tm) + [const((d, SSM_COLS))],
        out_specs=[row(SSM_INNER), row(SSM_CONV_DIM), row(SSM_GROUPS * LANES)],
        out_shape=[jax.ShapeDtypeStruct((t, SSM_INNER), F32), jax.ShapeDtypeStruct((t, SSM_CONV_DIM), F32),
                   jax.ShapeDtypeStruct((t, SSM_GROUPS * LANES), F32)],
        compiler_params=_cparams(("arbitrary",)),
        name="ssm_inproj",
    )(h2, gain, mod, mod, w_bf)


def _conv_kernel(x_ref, w_ref, b_ref, o_ref, ext_ref, *, ts):
    @pl.when(pl.program_id(2) == 0)
    def _():
        ext_ref[0:8, :] = jnp.zeros((8, ext_ref.shape[1]), F32)

    ext_ref[8:8 + ts, :] = x_ref[...]
    acc = b_ref[...] + w_ref[SSM_CONV - 1:SSM_CONV, :] * x_ref[...]
    for k in range(SSM_CONV - 1):
        acc = acc + w_ref[k:k + 1, :] * ext_ref[pl.ds(8 - (SSM_CONV - 1) + k, ts), :]
    o_ref[...] = acc * jax.nn.sigmoid(acc)
    ext_ref[0:8, :] = ext_ref[ts:ts + 8, :]


def _conv(xbc, conv_w, conv_b, batch, seq):
    t, c = xbc.shape
    ts, tc = CONV_ROWS, CONV_COLS
    ns = seq // ts
    return pl.pallas_call(
        functools.partial(_conv_kernel, ts=ts),
        grid=(c // tc, batch, ns),
        in_specs=[pl.BlockSpec((ts, tc), lambda j, b, i: (b * ns + i, j)),
                  pl.BlockSpec((SSM_CONV, tc), lambda j, b, i: (0, j)),
                  pl.BlockSpec((1, tc), lambda j, b, i: (0, j))],
        out_specs=pl.BlockSpec((ts, tc), lambda j, b, i: (b * ns + i, j)),
        out_shape=jax.ShapeDtypeStruct((t, c), F32),
        scratch_shapes=[pltpu.VMEM((ts + 8, tc), F32)],
        compiler_params=_cparams(("arbitrary",) * 3),
        name="ssm_conv",
    )(xbc, conv_w, conv_b.reshape(1, c))


def _ssd_kernel(x_ref, b_ref, c_ref, dt_ref, z_ref, dtb_ref, alog_ref, dsk_ref, ng_ref, tri_ref, ex_ref,
                o_ref, st_ref, *, chunk):
    @pl.when(pl.program_id(2) == 0)
    def _():
        st_ref[...] = jnp.zeros_like(st_ref)

    lane = lax.broadcasted_iota(I32, (1, LANES), 1)
    low = lane < SSM_HEAD_DIM
    x = x_ref[...]
    dt = _softplus(dt_ref[...] + dtb_ref[...])
    a = -jnp.exp(alog_ref[...])
    cs = _split_dot_left(tri_ref[...], dt * a, 3)
    ex = ex_ref[...]
    cs_x = _split_dot(cs, ex, 3)
    dt_x = _split_dot(dt, ex, 3)
    last_x = cs_x[chunk - 1:chunk, :]
    xd = x * dt_x
    bm = b_ref[...]
    cm = c_ref[...].astype(BF16)
    bt = bm.T.astype(BF16)
    cb = jnp.dot(cm, bt, preferred_element_type=F32)
    cs_t = cs.T
    row = lax.broadcasted_iota(I32, (chunk, chunk), 0)
    col = lax.broadcasted_iota(I32, (chunk, chunk), 1)
    causal = col <= row
    decay_in = jnp.exp(cs_x)
    w_end = jnp.exp(last_x - cs_x)
    decay_chunk = jnp.exp(last_x)
    ys = []
    for jp in range(SSM_GROUP_HEADS // 2):
        sl = slice(LANES * jp, LANES * (jp + 1))
        xd_pair = xd[:, sl]
        xd_bf = xd_pair.astype(BF16)
        yd = []
        for hh in range(2):
            j = 2 * jp + hh
            seg = cs[:, j:j + 1] - cs_t[j:j + 1, :]
            lm = jnp.exp(jnp.where(causal, seg, -jnp.inf))
            yd.append(jnp.dot((cb * lm).astype(BF16), xd_bf, preferred_element_type=F32))
        y = jnp.where(low, yd[0], yd[1])
        prev = st_ref[jp]
        y = y + jnp.dot(cm, prev.astype(BF16), preferred_element_type=F32) * decay_in[:, sl]
        new = jnp.dot(bt, (xd_pair * w_end[:, sl]).astype(BF16), preferred_element_type=F32)
        st_ref[jp] = prev * decay_chunk[:, sl] + new
        ys.append(y + x[:, sl] * dsk_ref[:, sl])
    y = jnp.concatenate(ys, axis=1)
    z = z_ref[...]
    y = y * (z * jax.nn.sigmoid(z))
    y = y * lax.rsqrt(jnp.mean(y * y, axis=-1, keepdims=True) + EPS)
    o_ref[...] = (y * ng_ref[...]).astype(BF16)


def _ssd(xbc, dt, z, dt_bias, a_log, d_skip, norm_g, batch, seq):
    t = xbc.shape[0]
    chunk = math.gcd(seq, SSM_CHUNK)
    nc = seq // chunk
    gw = SSM_GROUP_W
    xblocks = SSM_INNER // gw
    bblk0 = SSM_INNER // SSM_STATE
    cblk0 = bblk0 + SSM_GROUPS
    pad_g = lambda v: jnp.pad(v.reshape(SSM_GROUPS, 1, SSM_GROUP_HEADS),
                              ((0, 0), (0, 0), (0, LANES - SSM_GROUP_HEADS)))
    dsk = jnp.repeat(d_skip, SSM_HEAD_DIM).reshape(1, SSM_INNER)
    r = np.arange(chunk)
    tri = jnp.asarray(r[:, None] >= r[None, :], BF16)
    ex = jnp.asarray(np.arange(LANES)[:, None] == (np.arange(gw)[None, :] // SSM_HEAD_DIM), BF16)
    rowblk = lambda n, colfn: pl.BlockSpec((chunk, n), lambda b, g, c: (b * nc + c, colfn(g)))
    grp = pl.BlockSpec((None, 1, LANES), lambda b, g, c: (g, 0, 0))
    return pl.pallas_call(
        functools.partial(_ssd_kernel, chunk=chunk),
        grid=(batch, SSM_GROUPS, nc),
        in_specs=[rowblk(gw, lambda g: g), rowblk(SSM_STATE, lambda g: bblk0 + g),
                  rowblk(SSM_STATE, lambda g: cblk0 + g), rowblk(LANES, lambda g: g), rowblk(gw, lambda g: g),
                  grp, grp,
                  pl.BlockSpec((1, gw), lambda b, g, c: (0, g)), pl.BlockSpec((1, gw), lambda b, g, c: (0, g)),
                  pl.BlockSpec((chunk, chunk), lambda b, g, c: (0, 0)),
                  pl.BlockSpec((LANES, gw), lambda b, g, c: (0, 0))],
        out_specs=rowblk(gw, lambda g: g),
        out_shape=jax.ShapeDtypeStruct((t, SSM_INNER), BF16),
        scratch_shapes=[pltpu.VMEM((SSM_GROUP_HEADS // 2, SSM_STATE, LANES), F32)],
        compiler_params=_cparams(("arbitrary",) * 3),
        name="ssd",
    )(xbc, xbc, xbc, dt, z, pad_g(dt_bias), pad_g(a_log), dsk, norm_g.reshape(1, SSM_INNER), tri, ex)


def _attention_layer(h2, mod, layer, norm_mix, w_in, w_out, q_norm, k_norm, pe_k, pe_v, w1k, w2k, w1v, w2v,
                     batch, seq):
    tile2 = lambda v: jnp.concatenate([v, v]).reshape(1, LANES)
    bd = jnp.asarray(np.arange(LANES)[:, None] // HEAD_DIM == np.arange(LANES)[None, :] // HEAD_DIM, BF16)
    sb, nq, kvc, nkv, gates = _attn_inproj(h2, mod, layer, norm_mix.reshape(1, -1), _attn_weight_layout(w_in),
                                           tile2(q_norm), tile2(k_norm[1]), tile2(k_norm[2]), bd, seq)
    o_sb = _sb_attention(sb, batch, seq)
    kcmp, vcmp = _compress(kvc, batch, seq, pe_k, pe_v, w1k, w1v, w2k, w2v, tile2(k_norm[0]), bd)
    o_nsa = _nsa_attention(nq, kcmp, vcmp, nkv, gates, batch, seq)
    w_out_bf = w_out.astype(BF16)
    return _proj_residual([o_sb, o_nsa], [w_out_bf[:SB_W], w_out_bf[SB_W:]], h2, mod, layer, 2, seq)


def _mamba_layer(h2, mod, layer, norm_mix, w_in, conv_w, conv_b, dt_bias, a_log, d_skip, norm_g, w_out,
                 batch, seq):
    z, xbc, dt = _ssm_inproj(h2, mod, layer, norm_mix.reshape(1, -1), _ssm_weight_layout(w_in), seq)
    xbc = _conv(xbc, conv_w, conv_b, batch, seq)
    y = _ssd(xbc, dt, z, dt_bias, a_log, d_skip, norm_g, batch, seq)
    return _proj_residual([y], [w_out.astype(BF16)], h2, mod, layer, 2, seq)


def kernel(x, c, ada_w, ada_b, norm_mix, norm_ffn, attn_w_in, attn_w_out, nsa_q_norm, nsa_k_norm, cmp_pe_k, cmp_pe_v, cmp_w1_k, cmp_w2_k, cmp_w1_v, cmp_w2_v, ssm_w_in, ssm_conv_w, ssm_conv_b, ssm_dt_bias, ssm_a_log, ssm_d, ssm_norm, ssm_w_out, router_w, router_b, moe_w_gu, moe_b_gu, moe_w_down, moe_b_down):
    batch, seq, d = x.shape
    depth = ada_w.shape[0]
    c_pad = jnp.pad(c, ((0, 8 - batch % 8 if batch % 8 else 0), (0, 0)))
    mod = _adaln(c_pad, ada_w, ada_b)
    mod = mod.reshape(depth, c_pad.shape[0], 6, 1, d)
    h2 = x.reshape(batch * seq, d)
    for layer in range(depth):
        i = layer // 2
        if layer % 2 == 0:
            h2 = _attention_layer(h2, mod, layer, norm_mix[layer], attn_w_in[i], attn_w_out[i], nsa_q_norm[i],
                                  nsa_k_norm[i], cmp_pe_k[i], cmp_pe_v[i], cmp_w1_k[i], cmp_w2_k[i],
                                  cmp_w1_v[i], cmp_w2_v[i], batch, seq)
        else:
            h2 = _mamba_layer(h2, mod, layer, norm_mix[layer], ssm_w_in[i], ssm_conv_w[i], ssm_conv_b[i],
                              ssm_dt_bias[i], ssm_a_log[i], ssm_d[i], ssm_norm[i], ssm_w_out[i], batch, seq)
        h2 = _moe(h2, mod, layer, norm_ffn[layer].reshape(1, -1), router_w[layer], router_b[layer],
                  moe_w_gu[layer], moe_b_gu[layer], moe_w_down[layer], moe_b_down[layer], seq)
    return h2.reshape(batch, seq, d)
```

```python
import functools
import math

import numpy as np
import jax
import jax.numpy as jnp
from jax import lax
from jax.experimental import pallas as pl
from jax.experimental.pallas import tpu as pltpu

F32 = jnp.float32
BF16 = jnp.bfloat16
I32 = jnp.int32
HIGHEST = lax.Precision.HIGHEST

D_MODEL = 1024
HEAD_DIM = 64
LANES = 128
SB_HEADS = 8
NSA_HEADS = 8
NSA_KV_GROUPS = 2
NSA_REP = NSA_HEADS // NSA_KV_GROUPS
CMP_BLOCK = 32
CMP_STRIDE = 16
CMP_HIDDEN = 128
SLC_BLOCK = 64
SLC_TOPN = 16
WINDOW = 512
FORCE_SCORE = 1e6
SB_W = SB_HEADS * HEAD_DIM
NSA_QW = NSA_HEADS * HEAD_DIM
SSM_INNER = 2 * D_MODEL
SSM_HEAD_DIM = 64
SSM_HEADS = SSM_INNER // SSM_HEAD_DIM
SSM_GROUPS = 4
SSM_GROUP_HEADS = SSM_HEADS // SSM_GROUPS
SSM_GROUP_W = SSM_INNER // SSM_GROUPS
SSM_STATE = 128
SSM_CONV = 4
SSM_CHUNK = 256
SSM_CONV_DIM = SSM_INNER + 2 * SSM_GROUPS * SSM_STATE
N_EXPERTS = 32
TOP_K = 4
D_EXPERT = D_MODEL
SWIGLU_LIMIT = 7.0
SWIGLU_ALPHA = 1.702
EPS = 1e-6
NEG_BIG = -1e30
SB_DEAD = -110.0

VMEM_LIMIT = 56 * 1024 * 1024

ROW_TILE = 256
SB_TILE = 256
SB_PAIRS = 2
NSA_TQ = 256
NSA_TK = 256
MOE_TILE = 256
CONV_ROWS = 512
CONV_COLS = 1024


def _cparams(sem):
    return pltpu.CompilerParams(dimension_semantics=sem, vmem_limit_bytes=VMEM_LIMIT)


def _split_dot(a, m01, terms):
    out = None
    r = a
    for t in range(terms):
        hi = r.astype(BF16)
        d = jnp.dot(hi, m01, preferred_element_type=F32)
        out = d if out is None else out + d
        if t + 1 < terms:
            r = r - hi.astype(F32)
    return out


def _split_dot_left(m01, a, terms):
    out = None
    r = a
    for t in range(terms):
        hi = r.astype(BF16)
        d = jnp.dot(m01, hi, preferred_element_type=F32)
        out = d if out is None else out + d
        if t + 1 < terms:
            r = r - hi.astype(F32)
    return out


def _dot_nt(a, b):
    return lax.dot_general(a, b, (((1,), (1,)), ((), ())), preferred_element_type=F32)


def _softplus(x):
    return jnp.maximum(x, 0.0) + jnp.log(1.0 + jnp.exp(-jnp.abs(x)))


def _modulated_norm(h, gain, sc, sh):
    ms = jnp.mean(h * h, axis=-1, keepdims=True)
    y = h * lax.rsqrt(ms + EPS) * gain
    return y * (1.0 + sc) + sh


def _seg_rmsnorm(x, bd, gain):
    ssq = _split_dot(x * x, bd, 2)
    return x * lax.rsqrt(ssq * (1.0 / HEAD_DIM) + EPS) * gain


def _mod_specs(layer, whichs, tiles_per_batch):
    def mk(which):
        return pl.BlockSpec((None, None, None, 1, D_MODEL),
                            lambda i, *_: (layer, i // tiles_per_batch, which, 0, 0))
    return [mk(w) for w in whichs]


def _adaln_kernel(c_ref, w_ref, b_ref, o_ref):
    c = c_ref[...]
    s = c * jax.nn.sigmoid(c)
    o_ref[...] = jnp.dot(s, w_ref[...], precision=HIGHEST, preferred_element_type=F32) + b_ref[...]


def _adaln(c_pad, ada_w, ada_b):
    depth, d, n = ada_w.shape
    bp = c_pad.shape[0]
    tn = 1536
    return pl.pallas_call(
        _adaln_kernel,
        grid=(depth, n // tn),
        in_specs=[pl.BlockSpec((bp, d), lambda l, j: (0, 0)),
                  pl.BlockSpec((None, d, tn), lambda l, j: (l, 0, j)),
                  pl.BlockSpec((None, 1, tn), lambda l, j: (l, 0, j))],
        out_specs=pl.BlockSpec((None, bp, tn), lambda l, j: (l, 0, j)),
        out_shape=jax.ShapeDtypeStruct((depth, bp, n), F32),
        compiler_params=_cparams(("arbitrary", "arbitrary")),
        name="adaln",
    )(c_pad, ada_w, ada_b.reshape(depth, 1, n))


ATTN_SB0, ATTN_NQ0, ATTN_KVC0, ATTN_NKV0, ATTN_GT0, ATTN_COLS = 0, 1536, 2048, 2304, 3328, 3584


def _attn_weight_layout(w_in):
    d = w_in.shape[0]
    off = 3 * SB_W + NSA_QW
    gw = NSA_KV_GROUPS * HEAD_DIM
    parts = [w_in[:, :off + 2 * gw]]
    for kind in range(4):
        base = off + 2 * gw + kind * gw
        for g in range(NSA_KV_GROUPS):
            blk = w_in[:, base + g * HEAD_DIM: base + (g + 1) * HEAD_DIM]
            parts += [blk, blk]
    gbase = off + 6 * gw
    for g in range(NSA_KV_GROUPS):
        blk = w_in[:, gbase + g * NSA_REP * 3: gbase + (g + 1) * NSA_REP * 3]
        parts.append(jnp.pad(blk, ((0, 0), (0, LANES - NSA_REP * 3))))
    w = jnp.concatenate(parts, axis=1)
    assert w.shape == (d, ATTN_COLS)
    return w.astype(BF16)


def _attn_inproj_kernel(h_ref, gain_ref, sc_ref, sh_ref, w_ref, qg_ref, k1g_ref, k2g_ref, bd_ref,
                        sb_ref, nq_ref, kvc_ref, nkv_ref, gt_ref):
    u = _modulated_norm(h_ref[...], gain_ref[...], sc_ref[...], sh_ref[...]).astype(BF16)
    bd = bd_ref[...]
    scale = HEAD_DIM ** -0.5

    def proj(a, b):
        return jnp.dot(u, w_ref[:, a:b], preferred_element_type=F32)

    sb_ref[:, 0:SB_W] = (proj(0, SB_W) * scale).astype(BF16)
    sb_ref[:, SB_W:3 * SB_W] = proj(SB_W, 3 * SB_W).astype(BF16)
    for j in range(NSA_QW // LANES):
        x = proj(ATTN_NQ0 + LANES * j, ATTN_NQ0 + LANES * (j + 1))
        nq_ref[:, LANES * j:LANES * (j + 1)] = (_seg_rmsnorm(x, bd, qg_ref[...]) * scale).astype(BF16)
    kvc_ref[...] = proj(ATTN_KVC0, ATTN_NKV0).astype(BF16)
    for kind in range(4):
        for g in range(NSA_KV_GROUPS):
            c = (kind * NSA_KV_GROUPS + g) * LANES
            x = proj(ATTN_NKV0 + c, ATTN_NKV0 + c + LANES)
            if kind == 0:
                x = _seg_rmsnorm(x, bd, k1g_ref[...])
            elif kind == 2:
                x = _seg_rmsnorm(x, bd, k2g_ref[...])
            nkv_ref[:, c:c + LANES] = x.astype(BF16)
    gt_ref[...] = jax.nn.sigmoid(proj(ATTN_GT0, ATTN_COLS))


def _attn_inproj(h2, mod, layer, gain, w_bf, qg, k1g, k2g, bd, seq):
    t, d = h2.shape
    tm = ROW_TILE
    tpb = seq // tm
    row = lambda n: pl.BlockSpec((tm, n), lambda i: (i, 0))
    const = lambda shp: pl.BlockSpec(shp, lambda i: (0,) * len(shp))
    return pl.pallas_call(
        _attn_inproj_kernel,
        grid=(t // tm,),
        in_specs=[row(d), const((1, d))] + _mod_specs(layer, (1, 0), tpb)
                 + [const((d, ATTN_COLS)), const((1, LANES)), const((1, LANES)), const((1, LANES)),
                    const((LANES, LANES))],
        out_specs=[row(3 * SB_W), row(NSA_QW), row(2 * LANES), row(8 * LANES), row(2 * LANES)],
        out_shape=[jax.ShapeDtypeStruct((t, 3 * SB_W), BF16), jax.ShapeDtypeStruct((t, NSA_QW), BF16),
                   jax.ShapeDtypeStruct((t, 2 * LANES), BF16), jax.ShapeDtypeStruct((t, 8 * LANES), BF16),
                   jax.ShapeDtypeStruct((t, 2 * LANES), F32)],
        compiler_params=_cparams(("arbitrary",)),
        name="attn_inproj",
    )(h2, gain, mod, mod, w_bf, qg, k1g, k2g, bd)


def _sb_kernel(q_ref, k_ref, v_ref, u_ref, o_ref, *, tile):
    qi = pl.program_id(2)
    lane = lax.broadcasted_iota(I32, (1, LANES), 1)
    low = lane < HEAD_DIM
    n_heads = 2 * SB_PAIRS
    qs = []
    for p in range(SB_PAIRS):
        q = q_ref[:, LANES * p:LANES * (p + 1)]
        zero = jnp.zeros_like(q)
        qs += [jnp.where(low, q, zero), jnp.where(low, zero, q)]
    u = u_ref[...]
    row = lax.broadcasted_iota(I32, (tile, tile), 0)
    col = lax.broadcasted_iota(I32, (tile, tile), 1)
    causal = col < row

    def block(kb, accs, runs, masked):
        off = pl.multiple_of(kb * tile, tile)
        new_accs, new_runs = [], []
        for hh in range(n_heads):
            sl = slice(LANES * (hh // 2), LANES * (hh // 2 + 1))
            x = _dot_nt(qs[hh], k_ref[pl.ds(off, tile), sl])
            sp = _softplus(x)
            lk = -sp
            if masked:
                lk = jnp.where(causal, lk, 0.0)
            tot = _split_dot(lk, u, 2) + runs[hh]
            w = jnp.exp(x - sp + tot)
            if masked:
                w = jnp.where(causal, w, 0.0)
            pv = jnp.dot(w.astype(BF16), v_ref[pl.ds(off, tile), sl], preferred_element_type=F32)
            new_accs.append(accs[hh] + pv)
            new_runs.append(runs[hh] + jnp.sum(lk, axis=1, keepdims=True))
        return tuple(new_accs), tuple(new_runs)

    acc0 = (jnp.zeros((tile, LANES), F32),) * n_heads
    run0 = (jnp.zeros((tile, 1), F32),) * n_heads
    accs, runs = block(qi, acc0, run0, True)

    def alive(runs):
        top = runs[0]
        for r in runs[1:]:
            top = jnp.maximum(top, r)
        return jnp.max(top) > SB_DEAD

    def cond(carry):
        return (carry[0] < qi) & carry[1]

    def body(carry):
        i, _, accs, runs = carry
        accs, runs = block(qi - 1 - i, accs, runs, False)
        return i + 1, alive(runs), accs, runs

    _, _, accs, runs = lax.while_loop(cond, body, (jnp.int32(0), alive(runs), accs, runs))
    for p in range(SB_PAIRS):
        o_ref[:, LANES * p:LANES * (p + 1)] = jnp.where(low, accs[2 * p], accs[2 * p + 1]).astype(BF16)


def _sb_attention(sb, batch, seq):
    tile = SB_TILE
    nq = seq // tile
    width = SB_PAIRS * LANES
    pairs = SB_W // width
    r = np.arange(tile)
    u = jnp.asarray(r[:, None] > r[None, :], BF16)
    return pl.pallas_call(
        functools.partial(_sb_kernel, tile=tile),
        grid=(batch, pairs, nq),
        in_specs=[pl.BlockSpec((tile, width), lambda b, p, i: (b * nq + i, p)),
                  pl.BlockSpec((seq, width), lambda b, p, i: (b, pairs + p)),
                  pl.BlockSpec((seq, width), lambda b, p, i: (b, 2 * pairs + p)),
                  pl.BlockSpec((tile, tile), lambda b, p, i: (0, 0))],
        out_specs=pl.BlockSpec((tile, width), lambda b, p, i: (b * nq + i, p)),
        out_shape=jax.ShapeDtypeStruct((batch * seq, SB_W), BF16),
        compiler_params=_cparams(("arbitrary",) * 3),
        name="sb_attention",
    )(sb, sb, sb, u)


def _compress_kernel(hb_ref, wc_ref, pe_ref, w1k_ref, w1v_ref, w2k_ref, w2v_ref, kg_ref, bd_ref,
                     kc_ref, vc_ref):
    p = jnp.dot(hb_ref[...], wc_ref[...], preferred_element_type=F32)
    n_half = p.shape[0]
    for kind in range(2):
        w1 = (w1k_ref, w1v_ref)[kind][...]
        w2 = (w2k_ref, w2v_ref)[kind][...]
        pe_term = jnp.dot(pe_ref[kind], w1, precision=HIGHEST, preferred_element_type=F32)[0:1, :]
        for g in range(NSA_KV_GROUPS):
            c = (kind * NSA_KV_GROUPS + g) * 2 * LANES
            a = p[:, c:c + LANES]
            b = p[:, c + LANES:c + 2 * LANES]
            pre = a + pltpu.roll(b, n_half - 1, 0) + pe_term
            hid = pre * jax.nn.sigmoid(pre)
            out = jnp.dot(hid.astype(BF16), w2, preferred_element_type=F32)
            if kind == 0:
                kc_ref[g] = _seg_rmsnorm(out, bd_ref[...], kg_ref[...]).astype(BF16)
            else:
                vc_ref[g] = out.astype(BF16)


def _compress_weights(w1k, w1v):
    cols = []
    for kind, w1 in enumerate((w1k, w1v)):
        w1r = w1.reshape(2, CMP_STRIDE, HEAD_DIM, CMP_HIDDEN)
        for g in range(NSA_KV_GROUPS):
            seg = kind * NSA_KV_GROUPS + g
            for half in range(2):
                full = jnp.zeros((CMP_STRIDE, 2 * NSA_KV_GROUPS, HEAD_DIM, CMP_HIDDEN), F32)
                full = full.at[:, seg].set(w1r[half])
                cols.append(full.reshape(CMP_STRIDE * 2 * NSA_KV_GROUPS * HEAD_DIM, CMP_HIDDEN))
    return jnp.concatenate(cols, axis=1).astype(BF16)


def _compress(kvc, batch, seq, pe_k, pe_v, w1k, w1v, w2k, w2v, k0g, bd):
    n_half = seq // CMP_STRIDE
    width = CMP_STRIDE * 2 * LANES
    hb = kvc.reshape(batch, n_half, width)
    wc = _compress_weights(w1k, w1v)
    fk = CMP_BLOCK * HEAD_DIM
    pe = jnp.stack([jnp.broadcast_to(pe_k.reshape(1, fk), (8, fk)),
                    jnp.broadcast_to(pe_v.reshape(1, fk), (8, fk))])
    dup = lambda w: jnp.concatenate([w, w], axis=1).astype(BF16)
    const = lambda shp: pl.BlockSpec(shp, lambda b: (0,) * len(shp))
    out_sds = jax.ShapeDtypeStruct((batch, NSA_KV_GROUPS, n_half, LANES), BF16)
    out_spec = pl.BlockSpec((None, NSA_KV_GROUPS, n_half, LANES), lambda b: (b, 0, 0, 0))
    return pl.pallas_call(
        _compress_kernel,
        grid=(batch,),
        in_specs=[pl.BlockSpec((None, n_half, width), lambda b: (b, 0, 0)),
                  const(wc.shape), const(pe.shape), const(w1k.shape), const(w1v.shape),
                  const((CMP_HIDDEN, LANES)), const((CMP_HIDDEN, LANES)), const((1, LANES)),
                  const((LANES, LANES))],
        out_specs=[out_spec, out_spec],
        out_shape=[out_sds, out_sds],
        compiler_params=_cparams(("arbitrary",)),
        name="nsa_compress",
    )(hb, wc, pe, w1k, w1v, dup(w2k), dup(w2v), k0g, bd)


def _alibi_slopes(n):
    return [float(v) for v in np.asarray(2.0 ** (-8.0 * np.arange(1, n + 1) / n), np.float32)]


def _nsa_kernel(q_ref, kc_ref, vc_ref, ks_ref, vs_ref, kw_ref, vw_ref, gt_ref, ov_ref, e_ref, o_ref,
                *, tq, tk, seq):
    g = pl.program_id(1)
    qi = pl.program_id(2)
    t0 = qi * tq
    lane = lax.broadcasted_iota(I32, (1, LANES), 1)
    low = lane < HEAD_DIM
    tpos = t0 + lax.broadcasted_iota(I32, (tq, 1), 0)
    slopes = _alibi_slopes(NSA_HEADS)
    slope = [jnp.where(g == 0, slopes[r], slopes[NSA_REP + r]) for r in range(NSA_REP)]

    q = q_ref[...]
    qz = []
    for r in range(NSA_REP):
        blk = q[:, LANES * (r // 2):LANES * (r // 2 + 1)]
        zero = jnp.zeros_like(blk)
        qz.append(jnp.where(low, blk, zero) if r % 2 == 0 else jnp.where(low, zero, blk))

    kc = kc_ref[...]
    vc = vc_ref[...]
    n_cmp = kc.shape[0]
    cend = lax.broadcasted_iota(I32, (1, n_cmp), 1) * CMP_STRIDE + (CMP_BLOCK - 1)
    mask_c = cend <= tpos
    dist_c = (tpos - cend).astype(F32)
    o_c = []
    psum = jnp.zeros((tq, n_cmp), F32)
    for r in range(NSA_REP):
        s = _dot_nt(qz[r], kc) - slope[r] * dist_c
        s = jnp.where(mask_c, s, -jnp.inf)
        m = jnp.max(s, axis=1, keepdims=True)
        m = jnp.where(m == -jnp.inf, 0.0, m)
        p = jnp.exp(s - m)
        p = p / jnp.maximum(jnp.sum(p, axis=1, keepdims=True), 1e-30)
        o_c.append(jnp.dot(p.astype(BF16), vc, preferred_element_type=F32))
        psum = psum + p

    n_slc = seq // SLC_BLOCK
    p_slc = _split_dot(psum, ov_ref[...], 2)
    cur = jnp.right_shift(tpos, int(math.log2(SLC_BLOCK)))
    valid = lane * SLC_BLOCK <= tpos
    forced = (lane == 0) | (lane == cur) | (lane == cur - 1)
    score = jnp.where(valid, jnp.where(forced, FORCE_SCORE, p_slc), -FORCE_SCORE)
    sc_t = score.T[0:n_slc, :]
    jrow = lax.broadcasted_iota(I32, (n_slc, 1), 0)
    cnt = jnp.zeros((n_slc, tq), F32)
    for i in range(n_slc):
        ri = sc_t[i:i + 1, :]
        cnt = cnt + jnp.where(jrow > i, jnp.where(ri >= sc_t, 1.0, 0.0), jnp.where(ri > sc_t, 1.0, 0.0))
    sel_t = jnp.where(cnt < float(min(SLC_TOPN, n_slc)), 1.0, 0.0)
    if n_slc < LANES:
        sel_t = jnp.concatenate([sel_t, jnp.zeros((LANES - n_slc, tq), F32)], axis=0)
    sel = sel_t.T.astype(BF16)

    def sel_body(kt, carry):
        ms, ls, accs = carry
        off = pl.multiple_of(kt * tk, tk)
        k = ks_ref[pl.ds(off, tk), :]
        v = vs_ref[pl.ds(off, tk), :]
        member = jnp.dot(sel, e_ref[:, pl.ds(off, tk)], preferred_element_type=F32)
        pos = off + lax.broadcasted_iota(I32, (1, tk), 1)
        ok = jnp.where(pos <= tpos, member, 0.0) > 0.5
        dist = (tpos - pos).astype(F32)
        new_ms, new_ls, new_accs = [], [], []
        for r in range(NSA_REP):
            s = _dot_nt(qz[r], k) - slope[r] * dist
            s = jnp.where(ok, s, NEG_BIG)
            m_new = jnp.maximum(ms[r], jnp.max(s, axis=1, keepdims=True))
            alpha = jnp.exp(ms[r] - m_new)
            p = jnp.exp(s - m_new)
            new_ls.append(alpha * ls[r] + jnp.sum(p, axis=1, keepdims=True))
            new_accs.append(alpha * accs[r] + jnp.dot(p.astype(BF16), v, preferred_element_type=F32))
            new_ms.append(m_new)
        return tuple(new_ms), tuple(new_ls), tuple(new_accs)

    n_kt = (t0 + tq + tk - 1) // tk
    init = ((jnp.full((tq, 1), NEG_BIG, F32),) * NSA_REP, (jnp.zeros((tq, 1), F32),) * NSA_REP,
            (jnp.zeros((tq, LANES), F32),) * NSA_REP)
    ms, ls, accs = lax.fori_loop(0, n_kt, sel_body, init)
    o_s = [accs[r] / ls[r] for r in range(NSA_REP)]

    span = WINDOW + tq
    start = pl.multiple_of(jnp.maximum(t0 - WINDOW, 0), tq)
    kw = kw_ref[pl.ds(start, span), :]
    vw = vw_ref[pl.ds(start, span), :]
    pos = start + lax.broadcasted_iota(I32, (1, span), 1)
    ok_w = jnp.where(pos <= tpos, pos, -WINDOW - 1) > tpos - WINDOW
    dist_w = (tpos - pos).astype(F32)
    o_w = []
    for r in range(NSA_REP):
        s = _dot_nt(qz[r], kw) - slope[r] * dist_w
        s = jnp.where(ok_w, s, -jnp.inf)
        m = jnp.max(s, axis=1, keepdims=True)
        p = jnp.exp(s - m)
        p = p / jnp.maximum(jnp.sum(p, axis=1, keepdims=True), 1e-30)
        o_w.append(jnp.dot(p.astype(BF16), vw, preferred_element_type=F32))

    gt = gt_ref[...]
    outs = []
    for r in range(NSA_REP):
        outs.append(gt[:, 3 * r:3 * r + 1] * o_c[r] + gt[:, 3 * r + 1:3 * r + 2] * o_s[r]
                    + gt[:, 3 * r + 2:3 * r + 3] * o_w[r])
    for j in range(NSA_REP // 2):
        o_ref[:, LANES * j:LANES * (j + 1)] = jnp.where(low, outs[2 * j], outs[2 * j + 1]).astype(BF16)


def _nsa_attention(nq, kcmp, vcmp, nkv, gates, batch, seq):
    tq, tk = NSA_TQ, NSA_TK
    nqt = seq // tq
    n_cmp_rows = kcmp.shape[2]
    n_slc = seq // SLC_BLOCK
    cmp_start = np.arange(n_cmp_rows) * CMP_STRIDE
    slc_start = np.arange(LANES) * SLC_BLOCK
    overlap = ((cmp_start[:, None] < slc_start[None, :] + SLC_BLOCK)
               & (cmp_start[:, None] + CMP_BLOCK > slc_start[None, :])
               & (np.arange(LANES)[None, :] < n_slc))
    ov = jnp.asarray(overlap, BF16)
    expand = jnp.asarray(np.arange(LANES)[:, None] == (np.arange(seq)[None, :] // SLC_BLOCK), BF16)
    gw = NSA_REP * HEAD_DIM
    kv_spec = lambda kind: pl.BlockSpec((seq, LANES), lambda b, g, i: (b, kind * NSA_KV_GROUPS + g))
    cmp_spec = pl.BlockSpec((None, None, n_cmp_rows, LANES), lambda b, g, i: (b, g, 0, 0))
    return pl.pallas_call(
        functools.partial(_nsa_kernel, tq=tq, tk=tk, seq=seq),
        grid=(batch, NSA_KV_GROUPS, nqt),
        in_specs=[pl.BlockSpec((tq, gw), lambda b, g, i: (b * nqt + i, g)),
                  cmp_spec, cmp_spec, kv_spec(0), kv_spec(1), kv_spec(2), kv_spec(3),
                  pl.BlockSpec((tq, LANES), lambda b, g, i: (b * nqt + i, g)),
                  pl.BlockSpec(ov.shape, lambda b, g, i: (0, 0)),
                  pl.BlockSpec(expand.shape, lambda b, g, i: (0, 0))],
        out_specs=pl.BlockSpec((tq, gw), lambda b, g, i: (b * nqt + i, g)),
        out_shape=jax.ShapeDtypeStruct((batch * seq, NSA_QW), BF16),
        compiler_params=_cparams(("arbitrary",) * 3),
        name="nsa_attention",
    )(nq, kcmp, vcmp, nkv, nkv, nkv, nkv, gates, ov, expand)


def _proj_residual_kernel(*refs, n_in):
    x_refs, w_refs = refs[:n_in], refs[n_in:2 * n_in]
    h_ref, g_ref, o_ref = refs[2 * n_in:]
    acc = None
    for x_ref, w_ref in zip(x_refs, w_refs):
        d = jnp.dot(x_ref[...], w_ref[...], preferred_element_type=F32)
        acc = d if acc is None else acc + d
    o_ref[...] = h_ref[...] + g_ref[...] * acc


def _proj_residual(xs, ws, h2, mod, layer, which, seq):
    t, d = h2.shape
    tm = ROW_TILE
    n_in = len(xs)
    row = lambda n: pl.BlockSpec((tm, n), lambda i: (i, 0))
    const = lambda shp: pl.BlockSpec(shp, lambda i: (0,) * len(shp))
    return pl.pallas_call(
        functools.partial(_proj_residual_kernel, n_in=n_in),
        grid=(t // tm,),
        in_specs=[row(x.shape[1]) for x in xs] + [const(w.shape) for w in ws] + [row(d)]
                 + _mod_specs(layer, (which,), seq // tm),
        out_specs=row(d),
        out_shape=jax.ShapeDtypeStruct((t, d), F32),
        compiler_params=_cparams(("arbitrary",)),
        name="proj_residual",
    )(*xs, *ws, h2, mod)


def _router_kernel(h_ref, gain_ref, sc_ref, sh_ref, w_ref, b_ref, tri_ref,
                   u_ref, e_ref, p_ref, r_ref, c_ref, cnt_ref):
    @pl.when(pl.program_id(0) == 0)
    def _():
        cnt_ref[...] = jnp.zeros_like(cnt_ref)

    u = _modulated_norm(h_ref[...], gain_ref[...], sc_ref[...], sh_ref[...])
    u_ref[...] = u.astype(BF16)
    logits = jnp.dot(u, w_ref[...], precision=HIGHEST, preferred_element_type=F32) + b_ref[...]
    tm = logits.shape[0]
    lane = lax.broadcasted_iota(I32, (tm, LANES), 1)
    lane_f = lane.astype(F32)
    work = logits
    tops, idxs, hots = [], [], []
    for _ in range(TOP_K):
        m = jnp.max(work, axis=1, keepdims=True)
        idx = jnp.min(jnp.where(work == m, lane_f, float(LANES)), axis=1, keepdims=True)
        hot = lane_f == idx
        work = jnp.where(hot, -jnp.inf, work)
        tops.append(m)
        idxs.append(idx)
        hots.append(hot)
    ex = [jnp.exp(m - tops[0]) for m in tops]
    den = ex[0] + ex[1] + ex[2] + ex[3]
    any_hot = jnp.zeros((tm, LANES), F32)
    for hot in hots:
        any_hot = any_hot + jnp.where(hot, 1.0, 0.0)
    before = jnp.dot(tri_ref[...], any_hot.astype(BF16), preferred_element_type=F32) + cnt_ref[...]
    e_out = jnp.zeros((tm, LANES), F32)
    p_out = jnp.zeros((tm, LANES), F32)
    r_out = jnp.zeros((tm, LANES), F32)
    for k in range(TOP_K):
        rank = jnp.sum(jnp.where(hots[k], before, 0.0), axis=1, keepdims=True)
        e_out = jnp.where(lane == k, idxs[k], e_out)
        p_out = jnp.where(lane == k, ex[k] / den, p_out)
        r_out = jnp.where(lane == k, rank, r_out)
    e_ref[...] = e_out.astype(I32)
    p_ref[...] = p_out
    r_ref[...] = r_out.astype(I32)
    cnt = cnt_ref[...] + jnp.sum(any_hot, axis=0, keepdims=True)
    cnt_ref[...] = cnt
    c_ref[...] = jnp.broadcast_to(cnt, c_ref.shape).astype(I32)


def _router(h2, mod, layer, gain, w_router, b_router, seq):
    t, d = h2.shape
    tm = ROW_TILE
    wp = jnp.pad(w_router, ((0, 0), (0, LANES - N_EXPERTS)))
    bp = jnp.pad(b_router, (0, LANES - N_EXPERTS), constant_values=NEG_BIG).reshape(1, LANES)
    r = np.arange(tm)
    tri = jnp.asarray(r[:, None] > r[None, :], BF16)
    row = lambda n: pl.BlockSpec((tm, n), lambda i: (i, 0))
    const = lambda shp: pl.BlockSpec(shp, lambda i: (0,) * len(shp))
    return pl.pallas_call(
        _router_kernel,
        grid=(t // tm,),
        in_specs=[row(d), const((1, d))] + _mod_specs(layer, (4, 3), seq // tm)
                 + [const((d, LANES)), const((1, LANES)), const((tm, tm))],
        out_specs=[row(d), row(LANES), row(LANES), row(LANES), const((8, LANES))],
        out_shape=[jax.ShapeDtypeStruct((t, d), BF16), jax.ShapeDtypeStruct((t, LANES), I32),
                   jax.ShapeDtypeStruct((t, LANES), F32), jax.ShapeDtypeStruct((t, LANES), I32),
                   jax.ShapeDtypeStruct((8, LANES), I32)],
        scratch_shapes=[pltpu.VMEM((1, LANES), F32)],
        compiler_params=_cparams(("arbitrary",)),
        name="moe_router",
    )(h2, gain, mod, mod, wp, bp, tri)


def _expert_kernel(be_ref, nu_ref, x_ref, wgu_ref, bgu_ref, wd_ref, bd_ref, o_ref, wgu_bf, wd_bf):
    i = pl.program_id(0)
    prev = be_ref[jnp.maximum(i - 1, 0)]
    fresh = (i == 0) | (be_ref[i] != prev)

    @pl.when(fresh & (i < nu_ref[0]))
    def _():
        rows = 128

        def cast_gu(j, c):
            off = pl.multiple_of(j * rows, rows)
            wgu_bf[pl.ds(off, rows), :] = wgu_ref[pl.ds(off, rows), :].astype(BF16)
            wd_bf[pl.ds(off, rows), :] = wd_ref[pl.ds(off, rows), :].astype(BF16)
            return c

        lax.fori_loop(0, D_MODEL // rows, cast_gu, 0)

    @pl.when(i < nu_ref[0])
    def _():
        gu = jnp.dot(x_ref[...], wgu_bf[...], preferred_element_type=F32) + bgu_ref[...]
        gate = jnp.minimum(gu[:, :D_EXPERT], SWIGLU_LIMIT)
        up = jnp.clip(gu[:, D_EXPERT:], -SWIGLU_LIMIT, SWIGLU_LIMIT)
        act = (up + 1.0) * gate * jax.nn.sigmoid(SWIGLU_ALPHA * gate)
        y = jnp.dot(act.astype(BF16), wd_bf[...], preferred_element_type=F32) + bd_ref[...]
        o_ref[...] = y.astype(o_ref.dtype)

    @pl.when(i >= nu_ref[0])
    def _():
        o_ref[...] = jnp.zeros_like(o_ref)


def _experts(xs, blk_e, n_used, layer, w_gu, b_gu, w_down, b_down):
    n_slots, d = xs.shape
    tm = MOE_TILE
    _, ne, _, dgu = w_gu.shape
    grid_spec = pltpu.PrefetchScalarGridSpec(
        num_scalar_prefetch=2,
        grid=(n_slots // tm,),
        in_specs=[pl.BlockSpec((tm, d), lambda i, be, nu: (i, 0)),
                  pl.BlockSpec((None, None, d, dgu), lambda i, be, nu: (layer, be[i], 0, 0)),
                  pl.BlockSpec((None, 1, dgu), lambda i, be, nu: (be[i], 0, 0)),
                  pl.BlockSpec((None, None, D_EXPERT, d), lambda i, be, nu: (layer, be[i], 0, 0)),
                  pl.BlockSpec((None, 1, d), lambda i, be, nu: (be[i], 0, 0))],
        out_specs=pl.BlockSpec((tm, d), lambda i, be, nu: (i, 0)),
        scratch_shapes=[pltpu.VMEM((d, dgu), BF16), pltpu.VMEM((D_EXPERT, d), BF16)],
    )
    return pl.pallas_call(
        _expert_kernel,
        grid_spec=grid_spec,
        out_shape=jax.ShapeDtypeStruct((n_slots, d), BF16),
        compiler_params=_cparams(("arbitrary",)),
        name="moe_experts",
    )(blk_e, n_used, xs, w_gu, b_gu.reshape(ne, 1, dgu), w_down, b_down.reshape(ne, 1, d))


def _combine_kernel(y_ref, p_ref, h_ref, g_ref, o_ref):
    p = p_ref[...]
    acc = p[:, 0:1] * y_ref[0].astype(F32)
    for k in range(1, TOP_K):
        acc = acc + p[:, k:k + 1] * y_ref[k].astype(F32)
    o_ref[...] = h_ref[...] + g_ref[...] * acc


def _combine(yg, top_p, h2, mod, layer, seq):
    t, d = h2.shape
    tm = ROW_TILE
    row = lambda n: pl.BlockSpec((tm, n), lambda i: (i, 0))
    return pl.pallas_call(
        _combine_kernel,
        grid=(t // tm,),
        in_specs=[pl.BlockSpec((TOP_K, tm, d), lambda i: (0, i, 0)), row(LANES), row(d)]
                 + _mod_specs(layer, (5,), seq // tm),
        out_specs=row(d),
        out_shape=jax.ShapeDtypeStruct((t, d), F32),
        compiler_params=_cparams(("arbitrary",)),
        name="moe_combine",
    )(yg, top_p, h2, mod)


def _moe(h2, mod, layer, gain, w_router, b_router, w_gu, b_gu, w_down, b_down, seq):
    t, d = h2.shape
    tm = MOE_TILE
    u, top_e, top_p, rank, counts = _router(h2, mod, layer, gain, w_router, b_router, seq)
    counts = counts[0, :N_EXPERTS]
    padded = (counts + tm - 1) // tm * tm
    pad_end = jnp.cumsum(padded)
    pad_start = pad_end - padded
    eids = jnp.arange(N_EXPERTS, dtype=I32)
    hot = top_e[:, :TOP_K, None] == eids
    slot = jnp.sum(jnp.where(hot, pad_start, 0), axis=-1) + rank[:, :TOP_K]
    n_slots = t * TOP_K + N_EXPERTS * tm
    tok = jnp.broadcast_to(jnp.arange(t, dtype=I32)[:, None], (t, TOP_K))
    slot_tok = jnp.zeros((n_slots,), I32).at[slot.reshape(-1)].set(tok.reshape(-1))
    n_blocks = n_slots // tm
    blk_row0 = jnp.arange(n_blocks, dtype=I32) * tm
    blk_e = jnp.minimum(jnp.sum((pad_end[None, :] <= blk_row0[:, None]).astype(I32), axis=1), N_EXPERTS - 1)
    n_used = (pad_end[-1] // tm).astype(I32).reshape(1)
    xs = jnp.take(u, slot_tok, axis=0)
    ys = _experts(xs, blk_e, n_used, layer, w_gu, b_gu, w_down, b_down)
    yg = jnp.take(ys, slot.T.reshape(-1), axis=0).reshape(TOP_K, t, d)
    return _combine(yg, top_p, h2, mod, layer, seq)


SSM_Z0, SSM_X0, SSM_DT0, SSM_COLS = 0, SSM_INNER, SSM_INNER + SSM_CONV_DIM, SSM_INNER + SSM_CONV_DIM + SSM_GROUPS * LANES


def _ssm_weight_layout(w_in):
    parts = [w_in[:, :SSM_DT0]]
    for g in range(SSM_GROUPS):
        blk = w_in[:, SSM_DT0 + g * SSM_GROUP_HEADS: SSM_DT0 + (g + 1) * SSM_GROUP_HEADS]
        parts.append(jnp.pad(blk, ((0, 0), (0, LANES - SSM_GROUP_HEADS))))
    return jnp.concatenate(parts, axis=1).astype(BF16)


def _ssm_inproj_kernel(h_ref, gain_ref, sc_ref, sh_ref, w_ref, z_ref, xbc_ref, dt_ref):
    u = _modulated_norm(h_ref[...], gain_ref[...], sc_ref[...], sh_ref[...]).astype(BF16)
    step = 512
    for c in range(0, SSM_X0, step):
        z_ref[:, c:c + step] = jnp.dot(u, w_ref[:, c:c + step], preferred_element_type=F32)
    for c in range(0, SSM_CONV_DIM, step):
        xbc_ref[:, c:c + step] = jnp.dot(u, w_ref[:, SSM_X0 + c:SSM_X0 + c + step], preferred_element_type=F32)
    dt_ref[...] = jnp.dot(u, w_ref[:, SSM_DT0:SSM_COLS], preferred_element_type=F32)


def _ssm_inproj(h2, mod, layer, gain, w_bf, seq):
    t, d = h2.shape
    tm = ROW_TILE
    row = lambda n: pl.BlockSpec((tm, n), lambda i: (i, 0))
    const = lambda shp: pl.BlockSpec(shp, lambda i: (0,) * len(shp))
    return pl.pallas_call(
        _ssm_inproj_kernel,
        grid=(t // tm,),
        in_specs=[row(d), const((1, d))] + _mod_specs(layer, (1, 0), seq // tm) + [const((d, SSM_COLS))],
        out_specs=[row(SSM_INNER), row(SSM_CONV_DIM), row(SSM_GROUPS * LANES)],
        out_shape=[jax.ShapeDtypeStruct((t, SSM_INNER), F32), jax.ShapeDtypeStruct((t, SSM_CONV_DIM), F32),
                   jax.ShapeDtypeStruct((t, SSM_GROUPS * LANES), F32)],
        compiler_params=_cparams(("arbitrary",)),
        name="ssm_inproj",
    )(h2, gain, mod, mod, w_bf)


def _conv_kernel(x_ref, w_ref, b_ref, o_ref, ext_ref, *, ts):
    @pl.when(pl.program_id(2) == 0)
    def _():
        ext_ref[0:8, :] = jnp.zeros((8, ext_ref.shape[1]), F32)

    ext_ref[8:8 + ts, :] = x_ref[...]
    acc = b_ref[...] + w_ref[SSM_CONV - 1:SSM_CONV, :] * x_ref[...]
    for k in range(SSM_CONV - 1):
        acc = acc + w_ref[k:k + 1, :] * ext_ref[pl.ds(8 - (SSM_CONV - 1) + k, ts), :]
    o_ref[...] = acc * jax.nn.sigmoid(acc)
    ext_ref[0:8, :] = ext_ref[ts:ts + 8, :]


def _conv(xbc, conv_w, conv_b, batch, seq):
    t, c = xbc.shape
    ts, tc = CONV_ROWS, CONV_COLS
    ns = seq // ts
    return pl.pallas_call(
        functools.partial(_conv_kernel, ts=ts),
        grid=(c // tc, batch, ns),
        in_specs=[pl.BlockSpec((ts, tc), lambda j, b, i: (b * ns + i, j)),
                  pl.BlockSpec((SSM_CONV, tc), lambda j, b, i: (0, j)),
                  pl.BlockSpec((1, tc), lambda j, b, i: (0, j))],
        out_specs=pl.BlockSpec((ts, tc), lambda j, b, i: (b * ns + i, j)),
        out_shape=jax.ShapeDtypeStruct((t, c), F32),
        scratch_shapes=[pltpu.VMEM((ts + 8, tc), F32)],
        compiler_params=_cparams(("arbitrary",) * 3),
        name="ssm_conv",
    )(xbc, conv_w, conv_b.reshape(1, c))


def _ssd_kernel(x_ref, b_ref, c_ref, dt_ref, z_ref, dtb_ref, alog_ref, dsk_ref, ng_ref, tri_ref, ex_ref,
                o_ref, st_ref, *, chunk):
    @pl.when(pl.program_id(2) == 0)
    def _():
        st_ref[...] = jnp.zeros_like(st_ref)

    lane = lax.broadcasted_iota(I32, (1, LANES), 1)
    low = lane < SSM_HEAD_DIM
    x = x_ref[...]
    dt = _softplus(dt_ref[...] + dtb_ref[...])
    a = -jnp.exp(alog_ref[...])
    cs = _split_dot_left(tri_ref[...], dt * a, 3)
    ex = ex_ref[...]
    cs_x = _split_dot(cs, ex, 3)
    dt_x = _split_dot(dt, ex, 3)
    last_x = cs_x[chunk - 1:chunk, :]
    xd = x * dt_x
    bm = b_ref[...]
    cm = c_ref[...].astype(BF16)
    bt = bm.T.astype(BF16)
    cb = jnp.dot(cm, bt, preferred_element_type=F32)
    cs_t = cs.T
    row = lax.broadcasted_iota(I32, (chunk, chunk), 0)
    col = lax.broadcasted_iota(I32, (chunk, chunk), 1)
    causal = col <= row
    decay_in = jnp.exp(cs_x)
    w_end = jnp.exp(last_x - cs_x)
    decay_chunk = jnp.exp(last_x)
    ys = []
    for jp in range(SSM_GROUP_HEADS // 2):
        sl = slice(LANES * jp, LANES * (jp + 1))
        xd_pair = xd[:, sl]
        xd_bf = xd_pair.astype(BF16)
        yd = []
        for hh in range(2):
            j = 2 * jp + hh
            seg = cs[:, j:j + 1] - cs_t[j:j + 1, :]
            lm = jnp.exp(jnp.where(causal, seg, -jnp.inf))
            yd.append(jnp.dot((cb * lm).astype(BF16), xd_bf, preferred_element_type=F32))
        y = jnp.where(low, yd[0], yd[1])
        prev = st_ref[jp]
        y = y + jnp.dot(cm, prev.astype(BF16), preferred_element_type=F32) * decay_in[:, sl]
        new = jnp.dot(bt, (xd_pair * w_end[:, sl]).astype(BF16), preferred_element_type=F32)
        st_ref[jp] = prev * decay_chunk[:, sl] + new
        ys.append(y + x[:, sl] * dsk_ref[:, sl])
    y = jnp.concatenate(ys, axis=1)
    z = z_ref[...]
    y = y * (z * jax.nn.sigmoid(z))
    y = y * lax.rsqrt(jnp.mean(y * y, axis=-1, keepdims=True) + EPS)
    o_ref[...] = (y * ng_ref[...]).astype(BF16)


def _ssd(xbc, dt, z, dt_bias, a_log, d_skip, norm_g, batch, seq):
    t = xbc.shape[0]
    chunk = math.gcd(seq, SSM_CHUNK)
    nc = seq // chunk
    gw = SSM_GROUP_W
    xblocks = SSM_INNER // gw
    bblk0 = SSM_INNER // SSM_STATE
    cblk0 = bblk0 + SSM_GROUPS
    pad_g = lambda v: jnp.pad(v.reshape(SSM_GROUPS, 1, SSM_GROUP_HEADS),
                              ((0, 0), (0, 0), (0, LANES - SSM_GROUP_HEADS)))
    dsk = jnp.repeat(d_skip, SSM_HEAD_DIM).reshape(1, SSM_INNER)
    r = np.arange(chunk)
    tri = jnp.asarray(r[:, None] >= r[None, :], BF16)
    ex = jnp.asarray(np.arange(LANES)[:, None] == (np.arange(gw)[None, :] // SSM_HEAD_DIM), BF16)
    rowblk = lambda n, colfn: pl.BlockSpec((chunk, n), lambda b, g, c: (b * nc + c, colfn(g)))
    grp = pl.BlockSpec((None, 1, LANES), lambda b, g, c: (g, 0, 0))
    return pl.pallas_call(
        functools.partial(_ssd_kernel, chunk=chunk),
        grid=(batch, SSM_GROUPS, nc),
        in_specs=[rowblk(gw, lambda g: g), rowblk(SSM_STATE, lambda g: bblk0 + g),
                  rowblk(SSM_STATE, lambda g: cblk0 + g), rowblk(LANES, lambda g: g), rowblk(gw, lambda g: g),
                  grp, grp,
                  pl.BlockSpec((1, gw), lambda b, g, c: (0, g)), pl.BlockSpec((1, gw), lambda b, g, c: (0, g)),
                  pl.BlockSpec((chunk, chunk), lambda b, g, c: (0, 0)),
                  pl.BlockSpec((LANES, gw), lambda b, g, c: (0, 0))],
        out_specs=rowblk(gw, lambda g: g),
        out_shape=jax.ShapeDtypeStruct((t, SSM_INNER), BF16),
        scratch_shapes=[pltpu.VMEM((SSM_GROUP_HEADS // 2, SSM_STATE, LANES), F32)],
        compiler_params=_cparams(("arbitrary",) * 3),
        name="ssd",
    )(xbc, xbc, xbc, dt, z, pad_g(dt_bias), pad_g(a_log), dsk, norm_g.reshape(1, SSM_INNER), tri, ex)


def _attention_layer(h2, mod, layer, norm_mix, w_in, w_out, q_norm, k_norm, pe_k, pe_v, w1k, w2k, w1v, w2v,
                     batch, seq):
    tile2 = lambda v: jnp.concatenate([v, v]).reshape(1, LANES)
    bd = jnp.asarray(np.arange(LANES)[:, None] // HEAD_DIM == np.arange(LANES)[None, :] // HEAD_DIM, BF16)
    sb, nq, kvc, nkv, gates = _attn_inproj(h2, mod, layer, norm_mix.reshape(1, -1), _attn_weight_layout(w_in),
                                           tile2(q_norm), tile2(k_norm[1]), tile2(k_norm[2]), bd, seq)
    o_sb = _sb_attention(sb, batch, seq)
    kcmp, vcmp = _compress(kvc, batch, seq, pe_k, pe_v, w1k, w1v, w2k, w2v, tile2(k_norm[0]), bd)
    o_nsa = _nsa_attention(nq, kcmp, vcmp, nkv, gates, batch, seq)
    w_out_bf = w_out.astype(BF16)
    return _proj_residual([o_sb, o_nsa], [w_out_bf[:SB_W], w_out_bf[SB_W:]], h2, mod, layer, 2, seq)


def _mamba_layer(h2, mod, layer, norm_mix, w_in, conv_w, conv_b, dt_bias, a_log, d_skip, norm_g, w_out,
                 batch, seq):
    z, xbc, dt = _ssm_inproj(h2, mod, layer, norm_mix.reshape(1, -1), _ssm_weight_layout(w_in), seq)
    xbc = _conv(xbc, conv_w, conv_b, batch, seq)
    y = _ssd(xbc, dt, z, dt_bias, a_log, d_skip, norm_g, batch, seq)
    return _proj_residual([y], [w_out.astype(BF16)], h2, mod, layer, 2, seq)


def kernel(x, c, ada_w, ada_b, norm_mix, norm_ffn, attn_w_in, attn_w_out, nsa_q_norm, nsa_k_norm, cmp_pe_k, cmp_pe_v, cmp_w1_k, cmp_w2_k, cmp_w1_v, cmp_w2_v, ssm_w_in, ssm_conv_w, ssm_conv_b, ssm_dt_bias, ssm_a_log, ssm_d, ssm_norm, ssm_w_out, router_w, router_b, moe_w_gu, moe_b_gu, moe_w_down, moe_b_down):
    batch, seq, d = x.shape
    depth = ada_w.shape[0]
    c_pad = jnp.pad(c, ((0, 8 - batch % 8 if batch % 8 else 0), (0, 0)))
    mod = _adaln(c_pad, ada_w, ada_b)
    mod = mod.reshape(depth, c_pad.shape[0], 6, 1, d)
    h2 = x.reshape(batch * seq, d)
    for layer in range(depth):
        i = layer // 2
        if layer % 2 == 0:
            h2 = _attention_layer(h2, mod, layer, norm_mix[layer], attn_w_in[i], attn_w_out[i], nsa_q_norm[i],
                                  nsa_k_norm[i], cmp_pe_k[i], cmp_pe_v[i], cmp_w1_k[i], cmp_w2_k[i],
                                  cmp_w1_v[i], cmp_w2_v[i], batch, seq)
        else:
            h2 = _mamba_layer(h2, mod, layer, norm_mix[layer], ssm_w_in[i], ssm_conv_w[i], ssm_conv_b[i],
                              ssm_dt_bias[i], ssm_a_log[i], ssm_d[i], ssm_norm[i], ssm_w_out[i], batch, seq)
        h2 = _moe(h2, mod, layer, norm_ffn[layer].reshape(1, -1), router_w[layer], router_b[layer],
                  moe_w_gu, moe_b_gu[layer], moe_w_down, moe_b_down[layer], seq)
    return h2.reshape(batch, seq, d)
```

```python
import functools
import math

import numpy as np
import jax
import jax.numpy as jnp
from jax import lax
from jax.experimental import pallas as pl
from jax.experimental.pallas import tpu as pltpu

F32 = jnp.float32
BF16 = jnp.bfloat16
I32 = jnp.int32
HIGHEST = lax.Precision.HIGHEST

D_MODEL = 1024
HEAD_DIM = 64
LANES = 128
SB_HEADS = 8
NSA_HEADS = 8
NSA_KV_GROUPS = 2
NSA_REP = NSA_HEADS // NSA_KV_GROUPS
CMP_BLOCK = 32
CMP_STRIDE = 16
CMP_HIDDEN = 128
SLC_BLOCK = 64
SLC_TOPN = 16
WINDOW = 512
FORCE_SCORE = 1e6
SB_W = SB_HEADS * HEAD_DIM
NSA_QW = NSA_HEADS * HEAD_DIM
SSM_INNER = 2 * D_MODEL
SSM_HEAD_DIM = 64
SSM_HEADS = SSM_INNER // SSM_HEAD_DIM
SSM_GROUPS = 4
SSM_GROUP_HEADS = SSM_HEADS // SSM_GROUPS
SSM_GROUP_W = SSM_INNER // SSM_GROUPS
SSM_STATE = 128
SSM_CONV = 4
SSM_CHUNK = 256
SSM_CONV_DIM = SSM_INNER + 2 * SSM_GROUPS * SSM_STATE
N_EXPERTS = 32
TOP_K = 4
D_EXPERT = D_MODEL
SWIGLU_LIMIT = 7.0
SWIGLU_ALPHA = 1.702
EPS = 1e-6
NEG_BIG = -1e30
SB_DEAD = -110.0

VMEM_LIMIT = 56 * 1024 * 1024

ROW_TILE = 256
SB_TILE = 256
SB_PAIRS = 2
NSA_TQ = 256
NSA_TK = 256
MOE_TILE = 256
CONV_ROWS = 512
CONV_COLS = 1024


def _cparams(sem):
    return pltpu.CompilerParams(dimension_semantics=sem, vmem_limit_bytes=VMEM_LIMIT)


def _split_dot(a, m01, terms):
    out = None
    r = a
    for t in range(terms):
        hi = r.astype(BF16)
        d = jnp.dot(hi, m01, preferred_element_type=F32)
        out = d if out is None else out + d
        if t + 1 < terms:
            r = r - hi.astype(F32)
    return out


def _split_dot_left(m01, a, terms):
    out = None
    r = a
    for t in range(terms):
        hi = r.astype(BF16)
        d = jnp.dot(m01, hi, preferred_element_type=F32)
        out = d if out is None else out + d
        if t + 1 < terms:
            r = r - hi.astype(F32)
    return out


def _dot_nt(a, b):
    return lax.dot_general(a, b, (((1,), (1,)), ((), ())), preferred_element_type=F32)


def _softplus(x):
    return jnp.maximum(x, 0.0) + jnp.log(1.0 + jnp.exp(-jnp.abs(x)))


def _modulated_norm(h, gain, sc, sh):
    ms = jnp.mean(h * h, axis=-1, keepdims=True)
    y = h * lax.rsqrt(ms + EPS) * gain
    return y * (1.0 + sc) + sh


def _seg_rmsnorm(x, bd, gain):
    ssq = _split_dot(x * x, bd, 2)
    return x * lax.rsqrt(ssq * (1.0 / HEAD_DIM) + EPS) * gain


def _mod_specs(layer, whichs, tiles_per_batch):
    def mk(which):
        return pl.BlockSpec((None, None, None, 1, D_MODEL),
                            lambda i, *_: (layer, i // tiles_per_batch, which, 0, 0))
    return [mk(w) for w in whichs]


def _adaln_kernel(c_ref, w_ref, b_ref, o_ref):
    c = c_ref[...]
    s = c * jax.nn.sigmoid(c)
    o_ref[...] = jnp.dot(s, w_ref[...], precision=HIGHEST, preferred_element_type=F32) + b_ref[...]


def _adaln(c_pad, ada_w, ada_b):
    depth, d, n = ada_w.shape
    bp = c_pad.shape[0]
    tn = 1536
    return pl.pallas_call(
        _adaln_kernel,
        grid=(depth, n // tn),
        in_specs=[pl.BlockSpec((bp, d), lambda l, j: (0, 0)),
                  pl.BlockSpec((None, d, tn), lambda l, j: (l, 0, j)),
                  pl.BlockSpec((None, 1, tn), lambda l, j: (l, 0, j))],
        out_specs=pl.BlockSpec((None, bp, tn), lambda l, j: (l, 0, j)),
        out_shape=jax.ShapeDtypeStruct((depth, bp, n), F32),
        compiler_params=_cparams(("arbitrary", "arbitrary")),
        name="adaln",
    )(c_pad, ada_w, ada_b.reshape(depth, 1, n))


ATTN_SB0, ATTN_NQ0, ATTN_KVC0, ATTN_NKV0, ATTN_GT0, ATTN_COLS = 0, 1536, 2048, 2304, 3328, 3584


def _attn_weight_layout(w_in):
    d = w_in.shape[0]
    off = 3 * SB_W + NSA_QW
    gw = NSA_KV_GROUPS * HEAD_DIM
    parts = [w_in[:, :off + 2 * gw]]
    for kind in range(4):
        base = off + 2 * gw + kind * gw
        for g in range(NSA_KV_GROUPS):
            blk = w_in[:, base + g * HEAD_DIM: base + (g + 1) * HEAD_DIM]
            parts += [blk, blk]
    gbase = off + 6 * gw
    for g in range(NSA_KV_GROUPS):
        blk = w_in[:, gbase + g * NSA_REP * 3: gbase + (g + 1) * NSA_REP * 3]
        parts.append(jnp.pad(blk, ((0, 0), (0, LANES - NSA_REP * 3))))
    w = jnp.concatenate(parts, axis=1)
    assert w.shape == (d, ATTN_COLS)
    return w.astype(BF16)


def _attn_inproj_kernel(h_ref, gain_ref, sc_ref, sh_ref, w_ref, qg_ref, k1g_ref, k2g_ref, bd_ref,
                        sb_ref, nq_ref, kvc_ref, nkv_ref, gt_ref):
    u = _modulated_norm(h_ref[...], gain_ref[...], sc_ref[...], sh_ref[...]).astype(BF16)
    bd = bd_ref[...]
    scale = HEAD_DIM ** -0.5

    def proj(a, b):
        return jnp.dot(u, w_ref[:, a:b], preferred_element_type=F32)

    sb_ref[:, 0:SB_W] = (proj(0, SB_W) * scale).astype(BF16)
    sb_ref[:, SB_W:3 * SB_W] = proj(SB_W, 3 * SB_W).astype(BF16)
    for j in range(NSA_QW // LANES):
        x = proj(ATTN_NQ0 + LANES * j, ATTN_NQ0 + LANES * (j + 1))
        nq_ref[:, LANES * j:LANES * (j + 1)] = (_seg_rmsnorm(x, bd, qg_ref[...]) * scale).astype(BF16)
    kvc_ref[...] = proj(ATTN_KVC0, ATTN_NKV0).astype(BF16)
    for kind in range(4):
        for g in range(NSA_KV_GROUPS):
            c = (kind * NSA_KV_GROUPS + g) * LANES
            x = proj(ATTN_NKV0 + c, ATTN_NKV0 + c + LANES)
            if kind == 0:
                x = _seg_rmsnorm(x, bd, k1g_ref[...])
            elif kind == 2:
                x = _seg_rmsnorm(x, bd, k2g_ref[...])
            nkv_ref[:, c:c + LANES] = x.astype(BF16)
    gt_ref[...] = jax.nn.sigmoid(proj(ATTN_GT0, ATTN_COLS))


def _attn_inproj(h2, mod, layer, gain, w_bf, qg, k1g, k2g, bd, seq):
    t, d = h2.shape
    tm = ROW_TILE
    tpb = seq // tm
    row = lambda n: pl.BlockSpec((tm, n), lambda i: (i, 0))
    const = lambda shp: pl.BlockSpec(shp, lambda i: (0,) * len(shp))
    return pl.pallas_call(
        _attn_inproj_kernel,
        grid=(t // tm,),
        in_specs=[row(d), const((1, d))] + _mod_specs(layer, (1, 0), tpb)
                 + [const((d, ATTN_COLS)), const((1, LANES)), const((1, LANES)), const((1, LANES)),
                    const((LANES, LANES))],
        out_specs=[row(3 * SB_W), row(NSA_QW), row(2 * LANES), row(8 * LANES), row(2 * LANES)],
        out_shape=[jax.ShapeDtypeStruct((t, 3 * SB_W), BF16), jax.ShapeDtypeStruct((t, NSA_QW), BF16),
                   jax.ShapeDtypeStruct((t, 2 * LANES), BF16), jax.ShapeDtypeStruct((t, 8 * LANES), BF16),
                   jax.ShapeDtypeStruct((t, 2 * LANES), F32)],
        compiler_params=_cparams(("arbitrary",)),
        name="attn_inproj",
    )(h2, gain, mod, mod, w_bf, qg, k1g, k2g, bd)


def _sb_kernel(q_ref, k_ref, v_ref, u_ref, o_ref, *, tile):
    qi = pl.program_id(2)
    lane = lax.broadcasted_iota(I32, (1, LANES), 1)
    low = lane < HEAD_DIM
    n_heads = 2 * SB_PAIRS
    qs = []
    for p in range(SB_PAIRS):
        q = q_ref[:, LANES * p:LANES * (p + 1)]
        zero = jnp.zeros_like(q)
        qs += [jnp.where(low, q, zero), jnp.where(low, zero, q)]
    u = u_ref[...]
    row = lax.broadcasted_iota(I32, (tile, tile), 0)
    col = lax.broadcasted_iota(I32, (tile, tile), 1)
    causal = col < row

    def block(kb, accs, runs, masked):
        off = pl.multiple_of(kb * tile, tile)
        new_accs, new_runs = [], []
        for hh in range(n_heads):
            sl = slice(LANES * (hh // 2), LANES * (hh // 2 + 1))
            x = _dot_nt(qs[hh], k_ref[pl.ds(off, tile), sl])
            sp = _softplus(x)
            lk = -sp
            if masked:
                lk = jnp.where(causal, lk, 0.0)
            tot = _split_dot(lk, u, 2) + runs[hh]
            w = jnp.exp(x - sp + tot)
            if masked:
                w = jnp.where(causal, w, 0.0)
            pv = jnp.dot(w.astype(BF16), v_ref[pl.ds(off, tile), sl], preferred_element_type=F32)
            new_accs.append(accs[hh] + pv)
            new_runs.append(runs[hh] + jnp.sum(lk, axis=1, keepdims=True))
        return tuple(new_accs), tuple(new_runs)

    acc0 = (jnp.zeros((tile, LANES), F32),) * n_heads
    run0 = (jnp.zeros((tile, 1), F32),) * n_heads
    accs, runs = block(qi, acc0, run0, True)

    def alive(runs):
        top = runs[0]
        for r in runs[1:]:
            top = jnp.maximum(top, r)
        return jnp.max(top) > SB_DEAD

    def cond(carry):
        return (carry[0] < qi) & carry[1]

    def body(carry):
        i, _, accs, runs = carry
        accs, runs = block(qi - 1 - i, accs, runs, False)
        return i + 1, alive(runs), accs, runs

    _, _, accs, runs = lax.while_loop(cond, body, (jnp.int32(0), alive(runs), accs, runs))
    for p in range(SB_PAIRS):
        o_ref[:, LANES * p:LANES * (p + 1)] = jnp.where(low, accs[2 * p], accs[2 * p + 1]).astype(BF16)


def _sb_attention(sb, batch, seq):
    tile = SB_TILE
    nq = seq // tile
    width = SB_PAIRS * LANES
    pairs = SB_W // width
    r = np.arange(tile)
    u = jnp.asarray(r[:, None] > r[None, :], BF16)
    return pl.pallas_call(
        functools.partial(_sb_kernel, tile=tile),
        grid=(batch, pairs, nq),
        in_specs=[pl.BlockSpec((tile, width), lambda b, p, i: (b * nq + i, p)),
                  pl.BlockSpec((seq, width), lambda b, p, i: (b, pairs + p)),
                  pl.BlockSpec((seq, width), lambda b, p, i: (b, 2 * pairs + p)),
                  pl.BlockSpec((tile, tile), lambda b, p, i: (0, 0))],
        out_specs=pl.BlockSpec((tile, width), lambda b, p, i: (b * nq + i, p)),
        out_shape=jax.ShapeDtypeStruct((batch * seq, SB_W), BF16),
        compiler_params=_cparams(("arbitrary",) * 3),
        name="sb_attention",
    )(sb, sb, sb, u)


def _compress_kernel(hb_ref, wc_ref, pe_ref, w1k_ref, w1v_ref, w2k_ref, w2v_ref, kg_ref, bd_ref,
                     kc_ref, vc_ref):
    p = jnp.dot(hb_ref[...], wc_ref[...], preferred_element_type=F32)
    n_half = p.shape[0]
    for kind in range(2):
        w1 = (w1k_ref, w1v_ref)[kind][...]
        w2 = (w2k_ref, w2v_ref)[kind][...]
        pe_term = jnp.dot(pe_ref[kind], w1, precision=HIGHEST, preferred_element_type=F32)[0:1, :]
        for g in range(NSA_KV_GROUPS):
            c = (kind * NSA_KV_GROUPS + g) * 2 * LANES
            a = p[:, c:c + LANES]
            b = p[:, c + LANES:c + 2 * LANES]
            pre = a + pltpu.roll(b, n_half - 1, 0) + pe_term
            hid = pre * jax.nn.sigmoid(pre)
            out = jnp.dot(hid.astype(BF16), w2, preferred_element_type=F32)
            if kind == 0:
                kc_ref[g] = _seg_rmsnorm(out, bd_ref[...], kg_ref[...]).astype(BF16)
            else:
                vc_ref[g] = out.astype(BF16)


def _compress_weights(w1k, w1v):
    cols = []
    for kind, w1 in enumerate((w1k, w1v)):
        w1r = w1.reshape(2, CMP_STRIDE, HEAD_DIM, CMP_HIDDEN)
        for g in range(NSA_KV_GROUPS):
            seg = kind * NSA_KV_GROUPS + g
            for half in range(2):
                full = jnp.zeros((CMP_STRIDE, 2 * NSA_KV_GROUPS, HEAD_DIM, CMP_HIDDEN), F32)
                full = full.at[:, seg].set(w1r[half])
                cols.append(full.reshape(CMP_STRIDE * 2 * NSA_KV_GROUPS * HEAD_DIM, CMP_HIDDEN))
    return jnp.concatenate(cols, axis=1).astype(BF16)


def _compress(kvc, batch, seq, pe_k, pe_v, w1k, w1v, w2k, w2v, k0g, bd):
    n_half = seq // CMP_STRIDE
    width = CMP_STRIDE * 2 * LANES
    hb = kvc.reshape(batch, n_half, width)
    wc = _compress_weights(w1k, w1v)
    fk = CMP_BLOCK * HEAD_DIM
    pe = jnp.stack([jnp.broadcast_to(pe_k.reshape(1, fk), (8, fk)),
                    jnp.broadcast_to(pe_v.reshape(1, fk), (8, fk))])
    dup = lambda w: jnp.concatenate([w, w], axis=1).astype(BF16)
    const = lambda shp: pl.BlockSpec(shp, lambda b: (0,) * len(shp))
    out_sds = jax.ShapeDtypeStruct((batch, NSA_KV_GROUPS, n_half, LANES), BF16)
    out_spec = pl.BlockSpec((None, NSA_KV_GROUPS, n_half, LANES), lambda b: (b, 0, 0, 0))
    return pl.pallas_call(
        _compress_kernel,
        grid=(batch,),
        in_specs=[pl.BlockSpec((None, n_half, width), lambda b: (b, 0, 0)),
                  const(wc.shape), const(pe.shape), const(w1k.shape), const(w1v.shape),
                  const((CMP_HIDDEN, LANES)), const((CMP_HIDDEN, LANES)), const((1, LANES)),
                  const((LANES, LANES))],
        out_specs=[out_spec, out_spec],
        out_shape=[out_sds, out_sds],
        compiler_params=_cparams(("arbitrary",)),
        name="nsa_compress",
    )(hb, wc, pe, w1k, w1v, dup(w2k), dup(w2v), k0g, bd)


def _alibi_slopes(n):
    return [float(v) for v in np.asarray(2.0 ** (-8.0 * np.arange(1, n + 1) / n), np.float32)]


def _nsa_kernel(q_ref, kc_ref, vc_ref, ks_ref, vs_ref, kw_ref, vw_ref, gt_ref, ov_ref, e_ref, o_ref,
                *, tq, tk, seq):
    g = pl.program_id(1)
    qi = pl.program_id(2)
    t0 = qi * tq
    lane = lax.broadcasted_iota(I32, (1, LANES), 1)
    low = lane < HEAD_DIM
    tpos = t0 + lax.broadcasted_iota(I32, (tq, 1), 0)
    slopes = _alibi_slopes(NSA_HEADS)
    slope = [jnp.where(g == 0, slopes[r], slopes[NSA_REP + r]) for r in range(NSA_REP)]

    q = q_ref[...]
    qz = []
    for r in range(NSA_REP):
        blk = q[:, LANES * (r // 2):LANES * (r // 2 + 1)]
        zero = jnp.zeros_like(blk)
        qz.append(jnp.where(low, blk, zero) if r % 2 == 0 else jnp.where(low, zero, blk))

    kc = kc_ref[...]
    vc = vc_ref[...]
    n_cmp = kc.shape[0]
    cend = lax.broadcasted_iota(I32, (1, n_cmp), 1) * CMP_STRIDE + (CMP_BLOCK - 1)
    mask_c = cend <= tpos
    dist_c = (tpos - cend).astype(F32)
    o_c = []
    psum = jnp.zeros((tq, n_cmp), F32)
    for r in range(NSA_REP):
        s = _dot_nt(qz[r], kc) - slope[r] * dist_c
        s = jnp.where(mask_c, s, -jnp.inf)
        m = jnp.max(s, axis=1, keepdims=True)
        m = jnp.where(m == -jnp.inf, 0.0, m)
        p = jnp.exp(s - m)
        p = p / jnp.maximum(jnp.sum(p, axis=1, keepdims=True), 1e-30)
        o_c.append(jnp.dot(p.astype(BF16), vc, preferred_element_type=F32))
        psum = psum + p

    n_slc = seq // SLC_BLOCK
    p_slc = _split_dot(psum, ov_ref[...], 2)
    cur = jnp.right_shift(tpos, int(math.log2(SLC_BLOCK)))
    valid = lane * SLC_BLOCK <= tpos
    forced = (lane == 0) | (lane == cur) | (lane == cur - 1)
    score = jnp.where(valid, jnp.where(forced, FORCE_SCORE, p_slc), -FORCE_SCORE)
    sc_t = score.T[0:n_slc, :]
    jrow = lax.broadcasted_iota(I32, (n_slc, 1), 0)
    cnt = jnp.zeros((n_slc, tq), F32)
    for i in range(n_slc):
        ri = sc_t[i:i + 1, :]
        cnt = cnt + jnp.where(jrow > i, jnp.where(ri >= sc_t, 1.0, 0.0), jnp.where(ri > sc_t, 1.0, 0.0))
    sel_t = jnp.where(cnt < float(min(SLC_TOPN, n_slc)), 1.0, 0.0)
    if n_slc < LANES:
        sel_t = jnp.concatenate([sel_t, jnp.zeros((LANES - n_slc, tq), F32)], axis=0)
    sel = sel_t.T.astype(BF16)

    def sel_body(kt, carry):
        ms, ls, accs = carry
        off = pl.multiple_of(kt * tk, tk)
        k = ks_ref[pl.ds(off, tk), :]
        v = vs_ref[pl.ds(off, tk), :]
        member = jnp.dot(sel, e_ref[:, pl.ds(off, tk)], preferred_element_type=F32)
        pos = off + lax.broadcasted_iota(I32, (1, tk), 1)
        ok = jnp.where(pos <= tpos, member, 0.0) > 0.5
        dist = (tpos - pos).astype(F32)
        new_ms, new_ls, new_accs = [], [], []
        for r in range(NSA_REP):
            s = _dot_nt(qz[r], k) - slope[r] * dist
            s = jnp.where(ok, s, NEG_BIG)
            m_new = jnp.maximum(ms[r], jnp.max(s, axis=1, keepdims=True))
            alpha = jnp.exp(ms[r] - m_new)
            p = jnp.exp(s - m_new)
            new_ls.append(alpha * ls[r] + jnp.sum(p, axis=1, keepdims=True))
            new_accs.append(alpha * accs[r] + jnp.dot(p.astype(BF16), v, preferred_element_type=F32))
            new_ms.append(m_new)
        return tuple(new_ms), tuple(new_ls), tuple(new_accs)

    n_kt = (t0 + tq + tk - 1) // tk
    init = ((jnp.full((tq, 1), NEG_BIG, F32),) * NSA_REP, (jnp.zeros((tq, 1), F32),) * NSA_REP,
            (jnp.zeros((tq, LANES), F32),) * NSA_REP)
    ms, ls, accs = lax.fori_loop(0, n_kt, sel_body, init)
    o_s = [accs[r] / ls[r] for r in range(NSA_REP)]

    span = WINDOW + tq
    start = pl.multiple_of(jnp.maximum(t0 - WINDOW, 0), tq)
    kw = kw_ref[pl.ds(start, span), :]
    vw = vw_ref[pl.ds(start, span), :]
    pos = start + lax.broadcasted_iota(I32, (1, span), 1)
    ok_w = jnp.where(pos <= tpos, pos, -WINDOW - 1) > tpos - WINDOW
    dist_w = (tpos - pos).astype(F32)
    o_w = []
    for r in range(NSA_REP):
        s = _dot_nt(qz[r], kw) - slope[r] * dist_w
        s = jnp.where(ok_w, s, -jnp.inf)
        m = jnp.max(s, axis=1, keepdims=True)
        p = jnp.exp(s - m)
        p = p / jnp.maximum(jnp.sum(p, axis=1, keepdims=True), 1e-30)
        o_w.append(jnp.dot(p.astype(BF16), vw, preferred_element_type=F32))

    gt = gt_ref[...]
    outs = []
    for r in range(NSA_REP):
        outs.append(gt[:, 3 * r:3 * r + 1] * o_c[r] + gt[:, 3 * r + 1:3 * r + 2] * o_s[r]
                    + gt[:, 3 * r + 2:3 * r + 3] * o_w[r])
    for j in range(NSA_REP // 2):
        o_ref[:, LANES * j:LANES * (j + 1)] = jnp.where(low, outs[2 * j], outs[2 * j + 1]).astype(BF16)


def _nsa_attention(nq, kcmp, vcmp, nkv, gates, batch, seq):
    tq, tk = NSA_TQ, NSA_TK
    nqt = seq // tq
    n_cmp_rows = kcmp.shape[2]
    n_slc = seq // SLC_BLOCK
    cmp_start = np.arange(n_cmp_rows) * CMP_STRIDE
    slc_start = np.arange(LANES) * SLC_BLOCK
    overlap = ((cmp_start[:, None] < slc_start[None, :] + SLC_BLOCK)
               & (cmp_start[:, None] + CMP_BLOCK > slc_start[None, :])
               & (np.arange(LANES)[None, :] < n_slc))
    ov = jnp.asarray(overlap, BF16)
    expand = jnp.asarray(np.arange(LANES)[:, None] == (np.arange(seq)[None, :] // SLC_BLOCK), BF16)
    gw = NSA_REP * HEAD_DIM
    kv_spec = lambda kind: pl.BlockSpec((seq, LANES), lambda b, g, i: (b, kind * NSA_KV_GROUPS + g))
    cmp_spec = pl.BlockSpec((None, None, n_cmp_rows, LANES), lambda b, g, i: (b, g, 0, 0))
    return pl.pallas_call(
        functools.partial(_nsa_kernel, tq=tq, tk=tk, seq=seq),
        grid=(batch, NSA_KV_GROUPS, nqt),
        in_specs=[pl.BlockSpec((tq, gw), lambda b, g, i: (b * nqt + i, g)),
                  cmp_spec, cmp_spec, kv_spec(0), kv_spec(1), kv_spec(2), kv_spec(3),
                  pl.BlockSpec((tq, LANES), lambda b, g, i: (b * nqt + i, g)),
                  pl.BlockSpec(ov.shape, lambda b, g, i: (0, 0)),
                  pl.BlockSpec(expand.shape, lambda b, g, i: (0, 0))],
        out_specs=pl.BlockSpec((tq, gw), lambda b, g, i: (b * nqt + i, g)),
        out_shape=jax.ShapeDtypeStruct((batch * seq, NSA_QW), BF16),
        compiler_params=_cparams(("arbitrary",) * 3),
        name="nsa_attention",
    )(nq, kcmp, vcmp, nkv, nkv, nkv, nkv, gates, ov, expand)


def _proj_residual_kernel(*refs, n_in):
    x_refs, w_refs = refs[:n_in], refs[n_in:2 * n_in]
    h_ref, g_ref, o_ref = refs[2 * n_in:]
    acc = None
    for x_ref, w_ref in zip(x_refs, w_refs):
        d = jnp.dot(x_ref[...], w_ref[...], preferred_element_type=F32)
        acc = d if acc is None else acc + d
    o_ref[...] = h_ref[...] + g_ref[...] * acc


def _proj_residual(xs, ws, h2, mod, layer, which, seq):
    t, d = h2.shape
    tm = ROW_TILE
    n_in = len(xs)
    row = lambda n: pl.BlockSpec((tm, n), lambda i: (i, 0))
    const = lambda shp: pl.BlockSpec(shp, lambda i: (0,) * len(shp))
    return pl.pallas_call(
        functools.partial(_proj_residual_kernel, n_in=n_in),
        grid=(t // tm,),
        in_specs=[row(x.shape[1]) for x in xs] + [const(w.shape) for w in ws] + [row(d)]
                 + _mod_specs(layer, (which,), seq // tm),
        out_specs=row(d),
        out_shape=jax.ShapeDtypeStruct((t, d), F32),
        compiler_params=_cparams(("arbitrary",)),
        name="proj_residual",
    )(*xs, *ws, h2, mod)


def _router_kernel(h_ref, gain_ref, sc_ref, sh_ref, w_ref, b_ref, tri_ref,
                   u_ref, e_ref, p_ref, r_ref, c_ref, cnt_ref):
    @pl.when(pl.program_id(0) == 0)
    def _():
        cnt_ref[...] = jnp.zeros_like(cnt_ref)

    u = _modulated_norm(h_ref[...], gain_ref[...], sc_ref[...], sh_ref[...])
    u_ref[...] = u
    logits = jnp.dot(u, w_ref[...], precision=HIGHEST, preferred_element_type=F32) + b_ref[...]
    tm = logits.shape[0]
    lane = lax.broadcasted_iota(I32, (tm, LANES), 1)
    lane_f = lane.astype(F32)
    work = logits
    tops, idxs, hots = [], [], []
    for _ in range(TOP_K):
        m = jnp.max(work, axis=1, keepdims=True)
        idx = jnp.min(jnp.where(work == m, lane_f, float(LANES)), axis=1, keepdims=True)
        hot = lane_f == idx
        work = jnp.where(hot, -jnp.inf, work)
        tops.append(m)
        idxs.append(idx)
        hots.append(hot)
    ex = [jnp.exp(m - tops[0]) for m in tops]
    den = ex[0] + ex[1] + ex[2] + ex[3]
    any_hot = jnp.zeros((tm, LANES), F32)
    for hot in hots:
        any_hot = any_hot + jnp.where(hot, 1.0, 0.0)
    before = jnp.dot(tri_ref[...], any_hot.astype(BF16), preferred_element_type=F32) + cnt_ref[...]
    e_out = jnp.zeros((tm, LANES), F32)
    p_out = jnp.zeros((tm, LANES), F32)
    r_out = jnp.zeros((tm, LANES), F32)
    for k in range(TOP_K):
        rank = jnp.sum(jnp.where(hots[k], before, 0.0), axis=1, keepdims=True)
        e_out = jnp.where(lane == k, idxs[k], e_out)
        p_out = jnp.where(lane == k, ex[k] / den, p_out)
        r_out = jnp.where(lane == k, rank, r_out)
    e_ref[...] = e_out.astype(I32)
    p_ref[...] = p_out
    r_ref[...] = r_out.astype(I32)
    cnt = cnt_ref[...] + jnp.sum(any_hot, axis=0, keepdims=True)
    cnt_ref[...] = cnt
    c_ref[...] = jnp.broadcast_to(cnt, c_ref.shape).astype(I32)


def _router(h2, mod, layer, gain, w_router, b_router, seq):
    t, d = h2.shape
    tm = ROW_TILE
    wp = jnp.pad(w_router, ((0, 0), (0, LANES - N_EXPERTS)))
    bp = jnp.pad(b_router, (0, LANES - N_EXPERTS), constant_values=NEG_BIG).reshape(1, LANES)
    r = np.arange(tm)
    tri = jnp.asarray(r[:, None] > r[None, :], BF16)
    row = lambda n: pl.BlockSpec((tm, n), lambda i: (i, 0))
    const = lambda shp: pl.BlockSpec(shp, lambda i: (0,) * len(shp))
    return pl.pallas_call(
        _router_kernel,
        grid=(t // tm,),
        in_specs=[row(d), const((1, d))] + _mod_specs(layer, (4, 3), seq // tm)
                 + [const((d, LANES)), const((1, LANES)), const((tm, tm))],
        out_specs=[row(d), row(LANES), row(LANES), row(LANES), const((8, LANES))],
        out_shape=[jax.ShapeDtypeStruct((t, d), F32), jax.ShapeDtypeStruct((t, LANES), I32),
                   jax.ShapeDtypeStruct((t, LANES), F32), jax.ShapeDtypeStruct((t, LANES), I32),
                   jax.ShapeDtypeStruct((8, LANES), I32)],
        scratch_shapes=[pltpu.VMEM((1, LANES), F32)],
        compiler_params=_cparams(("arbitrary",)),
        name="moe_router",
    )(h2, gain, mod, mod, wp, bp, tri)


def _expert_kernel(be_ref, nu_ref, st_ref, u_hbm, wgu_ref, bgu_ref, wd_ref, bd_ref, o_ref,
                   wgu_bf, wd_bf, xbuf, sem):
    i = pl.program_id(0)
    tm = xbuf.shape[1]
    n_used = nu_ref[0]
    slot = lax.rem(i, 2)

    def row_copy(block, buf, r):
        tok = st_ref[block * tm + r]
        return pltpu.make_async_copy(u_hbm.at[pl.ds(tok, 1)], xbuf.at[buf, pl.ds(r, 1)], sem.at[buf])

    def start_rows(block, buf):
        def body(r, c):
            row_copy(block, buf, r).start()
            return c
        lax.fori_loop(0, tm, body, 0, unroll=8)

    def wait_rows(block, buf):
        def body(r, c):
            row_copy(block, buf, r).wait()
            return c
        lax.fori_loop(0, tm, body, 0, unroll=8)

    @pl.when((i == 0) & (n_used > 0))
    def _():
        start_rows(0, 0)

    @pl.when(i + 1 < n_used)
    def _():
        start_rows(i + 1, 1 - slot)

    prev = be_ref[jnp.maximum(i - 1, 0)]
    fresh = (i == 0) | (be_ref[i] != prev)

    @pl.when(fresh & (i < n_used))
    def _():
        rows = 128

        def cast_gu(j, c):
            off = pl.multiple_of(j * rows, rows)
            wgu_bf[pl.ds(off, rows), :] = wgu_ref[pl.ds(off, rows), :].astype(BF16)
            wd_bf[pl.ds(off, rows), :] = wd_ref[pl.ds(off, rows), :].astype(BF16)
            return c

        lax.fori_loop(0, D_MODEL // rows, cast_gu, 0)

    @pl.when(i < n_used)
    def _():
        wait_rows(i, slot)
        x = xbuf[slot].astype(BF16)
        gu = jnp.dot(x, wgu_bf[...], preferred_element_type=F32) + bgu_ref[...]
        gate = jnp.minimum(gu[:, :D_EXPERT], SWIGLU_LIMIT)
        up = jnp.clip(gu[:, D_EXPERT:], -SWIGLU_LIMIT, SWIGLU_LIMIT)
        act = (up + 1.0) * gate * jax.nn.sigmoid(SWIGLU_ALPHA * gate)
        y = jnp.dot(act.astype(BF16), wd_bf[...], preferred_element_type=F32) + bd_ref[...]
        o_ref[...] = y.astype(o_ref.dtype)

    @pl.when(i >= n_used)
    def _():
        o_ref[...] = jnp.zeros_like(o_ref)


def _experts(u, slot_tok, blk_e, n_used, layer, w_gu, b_gu, w_down, b_down):
    n_slots = slot_tok.shape[0]
    d = u.shape[1]
    tm = MOE_TILE
    _, ne, _, dgu = w_gu.shape
    grid_spec = pltpu.PrefetchScalarGridSpec(
        num_scalar_prefetch=3,
        grid=(n_slots // tm,),
        in_specs=[pl.BlockSpec(memory_space=pl.ANY),
                  pl.BlockSpec((None, None, d, dgu), lambda i, be, nu, st: (layer, be[i], 0, 0)),
                  pl.BlockSpec((None, 1, dgu), lambda i, be, nu, st: (be[i], 0, 0)),
                  pl.BlockSpec((None, None, D_EXPERT, d), lambda i, be, nu, st: (layer, be[i], 0, 0)),
                  pl.BlockSpec((None, 1, d), lambda i, be, nu, st: (be[i], 0, 0))],
        out_specs=pl.BlockSpec((tm, d), lambda i, be, nu, st: (i, 0)),
        scratch_shapes=[pltpu.VMEM((d, dgu), BF16), pltpu.VMEM((D_EXPERT, d), BF16),
                        pltpu.VMEM((2, tm, d), F32), pltpu.SemaphoreType.DMA((2,))],
    )
    return pl.pallas_call(
        _expert_kernel,
        grid_spec=grid_spec,
        out_shape=jax.ShapeDtypeStruct((n_slots, d), BF16),
        compiler_params=_cparams(("arbitrary",)),
        name="moe_experts",
    )(blk_e, n_used, slot_tok, u, w_gu, b_gu.reshape(ne, 1, dgu), w_down, b_down.reshape(ne, 1, d))


def _combine_kernel(y_ref, p_ref, h_ref, g_ref, o_ref):
    p = p_ref[...]
    acc = p[:, 0:1] * y_ref[0].astype(F32)
    for k in range(1, TOP_K):
        acc = acc + p[:, k:k + 1] * y_ref[k].astype(F32)
    o_ref[...] = h_ref[...] + g_ref[...] * acc


def _combine(yg, top_p, h2, mod, layer, seq):
    t, d = h2.shape
    tm = ROW_TILE
    row = lambda n: pl.BlockSpec((tm, n), lambda i: (i, 0))
    return pl.pallas_call(
        _combine_kernel,
        grid=(t // tm,),
        in_specs=[pl.BlockSpec((TOP_K, tm, d), lambda i: (0, i, 0)), row(LANES), row(d)]
                 + _mod_specs(layer, (5,), seq // tm),
        out_specs=row(d),
        out_shape=jax.ShapeDtypeStruct((t, d), F32),
        compiler_params=_cparams(("arbitrary",)),
        name="moe_combine",
    )(yg, top_p, h2, mod)


def _moe(h2, mod, layer, gain, w_router, b_router, w_gu, b_gu, w_down, b_down, seq):
    t, d = h2.shape
    tm = MOE_TILE
    u, top_e, top_p, rank, counts = _router(h2, mod, layer, gain, w_router, b_router, seq)
    counts = counts[0, :N_EXPERTS]
    padded = (counts + tm - 1) // tm * tm
    pad_end = jnp.cumsum(padded)
    pad_start = pad_end - padded
    eids = jnp.arange(N_EXPERTS, dtype=I32)
    hot = top_e[:, :TOP_K, None] == eids
    slot = jnp.sum(jnp.where(hot, pad_start, 0), axis=-1) + rank[:, :TOP_K]
    n_slots = t * TOP_K + N_EXPERTS * tm
    tok = jnp.broadcast_to(jnp.arange(t, dtype=I32)[:, None], (t, TOP_K))
    slot_tok = jnp.zeros((n_slots,), I32).at[slot.reshape(-1)].set(tok.reshape(-1))
    n_blocks = n_slots // tm
    blk_row0 = jnp.arange(n_blocks, dtype=I32) * tm
    blk_e = jnp.minimum(jnp.sum((pad_end[None, :] <= blk_row0[:, None]).astype(I32), axis=1), N_EXPERTS - 1)
    n_used = (pad_end[-1] // tm).astype(I32).reshape(1)
    ys = _experts(u, slot_tok, blk_e, n_used, layer, w_gu, b_gu, w_down, b_down)
    yg = ys.at[slot.T.reshape(-1)].get(mode="promise_in_bounds").reshape(TOP_K, t, d)
    return _combine(yg, top_p, h2, mod, layer, seq)


SSM_Z0, SSM_X0, SSM_DT0, SSM_COLS = 0, SSM_INNER, SSM_INNER + SSM_CONV_DIM, SSM_INNER + SSM_CONV_DIM + SSM_GROUPS * LANES


def _ssm_weight_layout(w_in):
    parts = [w_in[:, :SSM_DT0]]
    for g in range(SSM_GROUPS):
        blk = w_in[:, SSM_DT0 + g * SSM_GROUP_HEADS: SSM_DT0 + (g + 1) * SSM_GROUP_HEADS]
        parts.append(jnp.pad(blk, ((0, 0), (0, LANES - SSM_GROUP_HEADS))))
    return jnp.concatenate(parts, axis=1).astype(BF16)


def _ssm_inproj_kernel(h_ref, gain_ref, sc_ref, sh_ref, w_ref, z_ref, xbc_ref, dt_ref):
    u = _modulated_norm(h_ref[...], gain_ref[...], sc_ref[...], sh_ref[...]).astype(BF16)
    step = 512
    for c in range(0, SSM_X0, step):
        z_ref[:, c:c + step] = jnp.dot(u, w_ref[:, c:c + step], preferred_element_type=F32)
    for c in range(0, SSM_CONV_DIM, step):
        xbc_ref[:, c:c + step] = jnp.dot(u, w_ref[:, SSM_X0 + c:SSM_X0 + c + step], preferred_element_type=F32)
    dt_ref[...] = jnp.dot(u, w_ref[:, SSM_DT0:SSM_COLS], preferred_element_type=F32)


def _ssm_inproj(h2, mod, layer, gain, w_bf, seq):
    t, d = h2.shape
    tm = ROW_TILE
    row = lambda n: pl.BlockSpec((tm, n), lambda i: (i, 0))
    const = lambda shp: pl.BlockSpec(shp, lambda i: (0,) * len(shp))
    return pl.pallas_call(
        _ssm_inproj_kernel,
        grid=(t // tm,),
        in_specs=[row(d), const((1, d))] + _mod_specs(layer, (1, 0), seq // tm) + [const((d, SSM_COLS))],
        out_specs=[row(SSM_INNER), row(SSM_CONV_DIM), row(SSM_GROUPS * LANES)],
        out_shape=[jax.ShapeDtypeStruct((t, SSM_INNER), F32), jax.ShapeDtypeStruct((t, SSM_CONV_DIM), F32),
                   jax.ShapeDtypeStruct((t, SSM_GROUPS * LANES), F32)],
        compiler_params=_cparams(("arbitrary",)),
        name="ssm_inproj",
    )(h2, gain, mod, mod, w_bf)


def _conv_kernel(x_ref, w_ref, b_ref, o_ref, ext_ref, *, ts):
    @pl.when(pl.program_id(2) == 0)
    def _():
        ext_ref[0:8, :] = jnp.zeros((8, ext_ref.shape[1]), F32)

    ext_ref[8:8 + ts, :] = x_ref[...]
    acc = b_ref[...] + w_ref[SSM_CONV - 1:SSM_CONV, :] * x_ref[...]
    for k in range(SSM_CONV - 1):
        acc = acc + w_ref[k:k + 1, :] * ext_ref[pl.ds(8 - (SSM_CONV - 1) + k, ts), :]
    o_ref[...] = acc * jax.nn.sigmoid(acc)
    ext_ref[0:8, :] = ext_ref[ts:ts + 8, :]


def _conv(xbc, conv_w, conv_b, batch, seq):
    t, c = xbc.shape
    ts, tc = CONV_ROWS, CONV_COLS
    ns = seq // ts
    return pl.pallas_call(
        functools.partial(_conv_kernel, ts=ts),
        grid=(c // tc, batch, ns),
        in_specs=[pl.BlockSpec((ts, tc), lambda j, b, i: (b * ns + i, j)),
                  pl.BlockSpec((SSM_CONV, tc), lambda j, b, i: (0, j)),
                  pl.BlockSpec((1, tc), lambda j, b, i: (0, j))],
        out_specs=pl.BlockSpec((ts, tc), lambda j, b, i: (b * ns + i, j)),
        out_shape=jax.ShapeDtypeStruct((t, c), F32),
        scratch_shapes=[pltpu.VMEM((ts + 8, tc), F32)],
        compiler_params=_cparams(("arbitrary",) * 3),
        name="ssm_conv",
    )(xbc, conv_w, conv_b.reshape(1, c))


def _ssd_kernel(x_ref, b_ref, c_ref, dt_ref, z_ref, dtb_ref, alog_ref, dsk_ref, ng_ref, tri_ref, ex_ref,
                o_ref, st_ref, *, chunk):
    @pl.when(pl.program_id(2) == 0)
    def _():
        st_ref[...] = jnp.zeros_like(st_ref)

    lane = lax.broadcasted_iota(I32, (1, LANES), 1)
    low = lane < SSM_HEAD_DIM
    x = x_ref[...]
    dt = _softplus(dt_ref[...] + dtb_ref[...])
    a = -jnp.exp(alog_ref[...])
    cs = _split_dot_left(tri_ref[...], dt * a, 3)
    ex = ex_ref[...]
    cs_x = _split_dot(cs, ex, 3)
    dt_x = _split_dot(dt, ex, 3)
    last_x = cs_x[chunk - 1:chunk, :]
    xd = x * dt_x
    bm = b_ref[...]
    cm = c_ref[...].astype(BF16)
    bt = bm.T.astype(BF16)
    cb = jnp.dot(cm, bt, preferred_element_type=F32)
    cs_t = cs.T
    row = lax.broadcasted_iota(I32, (chunk, chunk), 0)
    col = lax.broadcasted_iota(I32, (chunk, chunk), 1)
    causal = col <= row
    decay_in = jnp.exp(cs_x)
    w_end = jnp.exp(last_x - cs_x)
    decay_chunk = jnp.exp(last_x)
    ys = []
    for jp in range(SSM_GROUP_HEADS // 2):
        sl = slice(LANES * jp, LANES * (jp + 1))
        xd_pair = xd[:, sl]
        xd_bf = xd_pair.astype(BF16)
        yd = []
        for hh in range(2):
            j = 2 * jp + hh
            seg = cs[:, j:j + 1] - cs_t[j:j + 1, :]
            lm = jnp.exp(jnp.where(causal, seg, -jnp.inf))
            yd.append(jnp.dot((cb * lm).astype(BF16), xd_bf, preferred_element_type=F32))
        y = jnp.where(low, yd[0], yd[1])
        prev = st_ref[jp]
        y = y + jnp.dot(cm, prev.astype(BF16), preferred_element_type=F32) * decay_in[:, sl]
        new = jnp.dot(bt, (xd_pair * w_end[:, sl]).astype(BF16), preferred_element_type=F32)
        st_ref[jp] = prev * decay_chunk[:, sl] + new
        ys.append(y + x[:, sl] * dsk_ref[:, sl])
    y = jnp.concatenate(ys, axis=1)
    z = z_ref[...]
    y = y * (z * jax.nn.sigmoid(z))
    y = y * lax.rsqrt(jnp.mean(y * y, axis=-1, keepdims=True) + EPS)
    o_ref[...] = (y * ng_ref[...]).astype(BF16)


def _ssd(xbc, dt, z, dt_bias, a_log, d_skip, norm_g, batch, seq):
    t = xbc.shape[0]
    chunk = math.gcd(seq, SSM_CHUNK)
    nc = seq // chunk
    gw = SSM_GROUP_W
    xblocks = SSM_INNER // gw
    bblk0 = SSM_INNER // SSM_STATE
    cblk0 = bblk0 + SSM_GROUPS
    pad_g = lambda v: jnp.pad(v.reshape(SSM_GROUPS, 1, SSM_GROUP_HEADS),
                              ((0, 0), (0, 0), (0, LANES - SSM_GROUP_HEADS)))
    dsk = jnp.repeat(d_skip, SSM_HEAD_DIM).reshape(1, SSM_INNER)
    r = np.arange(chunk)
    tri = jnp.asarray(r[:, None] >= r[None, :], BF16)
    ex = jnp.asarray(np.arange(LANES)[:, None] == (np.arange(gw)[None, :] // SSM_HEAD_DIM), BF16)
    rowblk = lambda n, colfn: pl.BlockSpec((chunk, n), lambda b, g, c: (b * nc + c, colfn(g)))
    grp = pl.BlockSpec((None, 1, LANES), lambda b, g, c: (g, 0, 0))
    return pl.pallas_call(
        functools.partial(_ssd_kernel, chunk=chunk),
        grid=(batch, SSM_GROUPS, nc),
        in_specs=[rowblk(gw, lambda g: g), rowblk(SSM_STATE, lambda g: bblk0 + g),
                  rowblk(SSM_STATE, lambda g: cblk0 + g), rowblk(LANES, lambda g: g), rowblk(gw, lambda g: g),
                  grp, grp,
                  pl.BlockSpec((1, gw), lambda b, g, c: (0, g)), pl.BlockSpec((1, gw), lambda b, g, c: (0, g)),
                  pl.BlockSpec((chunk, chunk), lambda b, g, c: (0, 0)),
                  pl.BlockSpec((LANES, gw), lambda b, g, c: (0, 0))],
        out_specs=rowblk(gw, lambda g: g),
        out_shape=jax.ShapeDtypeStruct((t, SSM_INNER), BF16),
        scratch_shapes=[pltpu.VMEM((SSM_GROUP_HEADS // 2, SSM_STATE, LANES), F32)],
        compiler_params=_cparams(("arbitrary",) * 3),
        name="ssd",
    )(xbc, xbc, xbc, dt, z, pad_g(dt_bias), pad_g(a_log), dsk, norm_g.reshape(1, SSM_INNER), tri, ex)


def _attention_layer(h2, mod, layer, norm_mix, w_in, w_out, q_norm, k_norm, pe_k, pe_v, w1k, w2k, w1v, w2v,
                     batch, seq):
    tile2 = lambda v: jnp.concatenate([v, v]).reshape(1, LANES)
    bd = jnp.asarray(np.arange(LANES)[:, None] // HEAD_DIM == np.arange(LANES)[None, :] // HEAD_DIM, BF16)
    sb, nq, kvc, nkv, gates = _attn_inproj(h2, mod, layer, norm_mix.reshape(1, -1), _attn_weight_layout(w_in),
                                           tile2(q_norm), tile2(k_norm[1]), tile2(k_norm[2]), bd, seq)
    o_sb = _sb_attention(sb, batch, seq)
    kcmp, vcmp = _compress(kvc, batch, seq, pe_k, pe_v, w1k, w1v, w2k, w2v, tile2(k_norm[0]), bd)
    o_nsa = _nsa_attention(nq, kcmp, vcmp, nkv, gates, batch, seq)
    w_out_bf = w_out.astype(BF16)
    return _proj_residual([o_sb, o_nsa], [w_out_bf[:SB_W], w_out_bf[SB_W:]], h2, mod, layer, 2, seq)


def _mamba_layer(h2, mod, layer, norm_mix, w_in, conv_w, conv_b, dt_bias, a_log, d_skip, norm_g, w_out,
                 batch, seq):
    z, xbc, dt = _ssm_inproj(h2, mod, layer, norm_mix.reshape(1, -1), _ssm_weight_layout(w_in), seq)
    xbc = _conv(xbc, conv_w, conv_b, batch, seq)
    y = _ssd(xbc, dt, z, dt_bias, a_log, d_skip, norm_g, batch, seq)
    return _proj_residual([y], [w_out.astype(BF16)], h2, mod, layer, 2, seq)


def kernel(x, c, ada_w, ada_b, norm_mix, norm_ffn, attn_w_in, attn_w_out, nsa_q_norm, nsa_k_norm, cmp_pe_k, cmp_pe_v, cmp_w1_k, cmp_w2_k, cmp_w1_v, cmp_w2_v, ssm_w_in, ssm_conv_w, ssm_conv_b, ssm_dt_bias, ssm_a_log, ssm_d, ssm_norm, ssm_w_out, router_w, router_b, moe_w_gu, moe_b_gu, moe_w_down, moe_b_down):
    batch, seq, d = x.shape
    depth = ada_w.shape[0]
    c_pad = jnp.pad(c, ((0, 8 - batch % 8 if batch % 8 else 0), (0, 0)))
    mod = _adaln(c_pad, ada_w, ada_b)
    mod = mod.reshape(depth, c_pad.shape[0], 6, 1, d)
    h2 = x.reshape(batch * seq, d)
    for layer in range(depth):
        i = layer // 2
        if layer % 2 == 0:
            h2 = _attention_layer(h2, mod, layer, norm_mix[layer], attn_w_in[i], attn_w_out[i], nsa_q_norm[i],
                                  nsa_k_norm[i], cmp_pe_k[i], cmp_pe_v[i], cmp_w1_k[i], cmp_w2_k[i],
                                  cmp_w1_v[i], cmp_w2_v[i], batch, seq)
        else:
            h2 = _mamba_layer(h2, mod, layer, norm_mix[layer], ssm_w_in[i], ssm_conv_w[i], ssm_conv_b[i],
                              ssm_dt_bias[i], ssm_a_log[i], ssm_d[i], ssm_norm[i], ssm_w_out[i], batch, seq)
        h2 = _moe(h2, mod, layer, norm_ffn[layer].reshape(1, -1), router_w[layer], router_b[layer],
                  moe_w_gu, moe_b_gu[layer], moe_w_down, moe_b_down[layer], seq)
    return h2.reshape(batch, seq, d)
```

```python
import functools
import math

import numpy as np
import jax
import jax.numpy as jnp
from jax import lax
from jax.experimental import pallas as pl
from jax.experimental.pallas import tpu as pltpu

F32 = jnp.float32
BF16 = jnp.bfloat16
I32 = jnp.int32
HIGHEST = lax.Precision.HIGHEST

D_MODEL = 1024
HEAD_DIM = 64
LANES = 128
SB_HEADS = 8
NSA_HEADS = 8
NSA_KV_GROUPS = 2
NSA_REP = NSA_HEADS // NSA_KV_GROUPS
CMP_BLOCK = 32
CMP_STRIDE = 16
CMP_HIDDEN = 128
SLC_BLOCK = 64
SLC_TOPN = 16
WINDOW = 512
FORCE_SCORE = 1e6
SB_W = SB_HEADS * HEAD_DIM
NSA_QW = NSA_HEADS * HEAD_DIM
SSM_INNER = 2 * D_MODEL
SSM_HEAD_DIM = 64
SSM_HEADS = SSM_INNER // SSM_HEAD_DIM
SSM_GROUPS = 4
SSM_GROUP_HEADS = SSM_HEADS // SSM_GROUPS
SSM_GROUP_W = SSM_INNER // SSM_GROUPS
SSM_STATE = 128
SSM_CONV = 4
SSM_CHUNK = 256
SSM_CONV_DIM = SSM_INNER + 2 * SSM_GROUPS * SSM_STATE
N_EXPERTS = 32
TOP_K = 4
D_EXPERT = D_MODEL
SWIGLU_LIMIT = 7.0
SWIGLU_ALPHA = 1.702
EPS = 1e-6
NEG_BIG = -1e30
SB_DEAD = -110.0

VMEM_LIMIT = 56 * 1024 * 1024

ROW_TILE = 256
SB_TILE = 256
SB_PAIRS = 2
NSA_TQ = 256
NSA_TK = 256
MOE_TILE = 256
CONV_ROWS = 512
CONV_COLS = 1024


def _cparams(sem):
    return pltpu.CompilerParams(dimension_semantics=sem, vmem_limit_bytes=VMEM_LIMIT)


def _split_dot(a, m01, terms):
    out = None
    r = a
    for t in range(terms):
        hi = r.astype(BF16)
        d = jnp.dot(hi, m01, preferred_element_type=F32)
        out = d if out is None else out + d
        if t + 1 < terms:
            r = r - hi.astype(F32)
    return out


def _split_dot_left(m01, a, terms):
    out = None
    r = a
    for t in range(terms):
        hi = r.astype(BF16)
        d = jnp.dot(m01, hi, preferred_element_type=F32)
        out = d if out is None else out + d
        if t + 1 < terms:
            r = r - hi.astype(F32)
    return out


def _dot_nt(a, b):
    return lax.dot_general(a, b, (((1,), (1,)), ((), ())), preferred_element_type=F32)


def _softplus(x):
    return jnp.maximum(x, 0.0) + jnp.log(1.0 + jnp.exp(-jnp.abs(x)))


def _modulated_norm(h, gain, sc, sh):
    ms = jnp.mean(h * h, axis=-1, keepdims=True)
    y = h * lax.rsqrt(ms + EPS) * gain
    return y * (1.0 + sc) + sh


def _seg_rmsnorm(x, bd, gain):
    ssq = _split_dot(x * x, bd, 2)
    return x * lax.rsqrt(ssq * (1.0 / HEAD_DIM) + EPS) * gain


def _mod_specs(layer, whichs, tiles_per_batch):
    def mk(which):
        return pl.BlockSpec((None, None, None, 1, D_MODEL),
                            lambda i, *_: (layer, i // tiles_per_batch, which, 0, 0))
    return [mk(w) for w in whichs]


def _adaln_kernel(c_ref, w_ref, b_ref, o_ref):
    c = c_ref[...]
    s = c * jax.nn.sigmoid(c)
    o_ref[...] = jnp.dot(s, w_ref[...], precision=HIGHEST, preferred_element_type=F32) + b_ref[...]


def _adaln(c_pad, ada_w, ada_b):
    depth, d, n = ada_w.shape
    bp = c_pad.shape[0]
    tn = 1536
    return pl.pallas_call(
        _adaln_kernel,
        grid=(depth, n // tn),
        in_specs=[pl.BlockSpec((bp, d), lambda l, j: (0, 0)),
                  pl.BlockSpec((None, d, tn), lambda l, j: (l, 0, j)),
                  pl.BlockSpec((None, 1, tn), lambda l, j: (l, 0, j))],
        out_specs=pl.BlockSpec((None, bp, tn), lambda l, j: (l, 0, j)),
        out_shape=jax.ShapeDtypeStruct((depth, bp, n), F32),
        compiler_params=_cparams(("arbitrary", "arbitrary")),
        name="adaln",
    )(c_pad, ada_w, ada_b.reshape(depth, 1, n))


ATTN_SB0, ATTN_NQ0, ATTN_KVC0, ATTN_NKV0, ATTN_GT0, ATTN_COLS = 0, 1536, 2048, 2304, 3328, 3584


def _attn_weight_layout(w_in):
    d = w_in.shape[0]
    off = 3 * SB_W + NSA_QW
    gw = NSA_KV_GROUPS * HEAD_DIM
    parts = [w_in[:, :off + 2 * gw]]
    for kind in range(4):
        base = off + 2 * gw + kind * gw
        for g in range(NSA_KV_GROUPS):
            blk = w_in[:, base + g * HEAD_DIM: base + (g + 1) * HEAD_DIM]
            parts += [blk, blk]
    gbase = off + 6 * gw
    for g in range(NSA_KV_GROUPS):
        blk = w_in[:, gbase + g * NSA_REP * 3: gbase + (g + 1) * NSA_REP * 3]
        parts.append(jnp.pad(blk, ((0, 0), (0, LANES - NSA_REP * 3))))
    w = jnp.concatenate(parts, axis=1)
    assert w.shape == (d, ATTN_COLS)
    return w.astype(BF16)


def _attn_inproj_kernel(h_ref, gain_ref, sc_ref, sh_ref, w_ref, qg_ref, k1g_ref, k2g_ref, bd_ref,
                        sb_ref, nq_ref, kvc_ref, nkv_ref, gt_ref):
    u = _modulated_norm(h_ref[...], gain_ref[...], sc_ref[...], sh_ref[...]).astype(BF16)
    bd = bd_ref[...]
    scale = HEAD_DIM ** -0.5

    def proj(a, b):
        return jnp.dot(u, w_ref[:, a:b], preferred_element_type=F32)

    sb_ref[:, 0:SB_W] = (proj(0, SB_W) * scale).astype(BF16)
    sb_ref[:, SB_W:3 * SB_W] = proj(SB_W, 3 * SB_W).astype(BF16)
    for j in range(NSA_QW // LANES):
        x = proj(ATTN_NQ0 + LANES * j, ATTN_NQ0 + LANES * (j + 1))
        nq_ref[:, LANES * j:LANES * (j + 1)] = (_seg_rmsnorm(x, bd, qg_ref[...]) * scale).astype(BF16)
    kvc_ref[...] = proj(ATTN_KVC0, ATTN_NKV0).astype(BF16)
    for kind in range(4):
        for g in range(NSA_KV_GROUPS):
            c = (kind * NSA_KV_GROUPS + g) * LANES
            x = proj(ATTN_NKV0 + c, ATTN_NKV0 + c + LANES)
            if kind == 0:
                x = _seg_rmsnorm(x, bd, k1g_ref[...])
            elif kind == 2:
                x = _seg_rmsnorm(x, bd, k2g_ref[...])
            nkv_ref[:, c:c + LANES] = x.astype(BF16)
    gt_ref[...] = jax.nn.sigmoid(proj(ATTN_GT0, ATTN_COLS))


def _attn_inproj(h2, mod, layer, gain, w_bf, qg, k1g, k2g, bd, seq):
    t, d = h2.shape
    tm = ROW_TILE
    tpb = seq // tm
    row = lambda n: pl.BlockSpec((tm, n), lambda i: (i, 0))
    const = lambda shp: pl.BlockSpec(shp, lambda i: (0,) * len(shp))
    return pl.pallas_call(
        _attn_inproj_kernel,
        grid=(t // tm,),
        in_specs=[row(d), const((1, d))] + _mod_specs(layer, (1, 0), tpb)
                 + [const((d, ATTN_COLS)), const((1, LANES)), const((1, LANES)), const((1, LANES)),
                    const((LANES, LANES))],
        out_specs=[row(3 * SB_W), row(NSA_QW), row(2 * LANES), row(8 * LANES), row(2 * LANES)],
        out_shape=[jax.ShapeDtypeStruct((t, 3 * SB_W), BF16), jax.ShapeDtypeStruct((t, NSA_QW), BF16),
                   jax.ShapeDtypeStruct((t, 2 * LANES), BF16), jax.ShapeDtypeStruct((t, 8 * LANES), BF16),
                   jax.ShapeDtypeStruct((t, 2 * LANES), F32)],
        compiler_params=_cparams(("arbitrary",)),
        name="attn_inproj",
    )(h2, gain, mod, mod, w_bf, qg, k1g, k2g, bd)


def _sb_kernel(q_ref, k_ref, v_ref, u_ref, o_ref, *, tile):
    qi = pl.program_id(2)
    lane = lax.broadcasted_iota(I32, (1, LANES), 1)
    low = lane < HEAD_DIM
    n_heads = 2 * SB_PAIRS
    qs = []
    for p in range(SB_PAIRS):
        q = q_ref[:, LANES * p:LANES * (p + 1)]
        zero = jnp.zeros_like(q)
        qs += [jnp.where(low, q, zero), jnp.where(low, zero, q)]
    u = u_ref[...]
    row = lax.broadcasted_iota(I32, (tile, tile), 0)
    col = lax.broadcasted_iota(I32, (tile, tile), 1)
    causal = col < row

    def block(kb, accs, runs, masked):
        off = pl.multiple_of(kb * tile, tile)
        new_accs, new_runs = [], []
        for hh in range(n_heads):
            sl = slice(LANES * (hh // 2), LANES * (hh // 2 + 1))
            x = _dot_nt(qs[hh], k_ref[pl.ds(off, tile), sl])
            sp = _softplus(x)
            lk = -sp
            if masked:
                lk = jnp.where(causal, lk, 0.0)
            tot = _split_dot(lk, u, 2) + runs[hh]
            w = jnp.exp(x - sp + tot)
            if masked:
                w = jnp.where(causal, w, 0.0)
            pv = jnp.dot(w.astype(BF16), v_ref[pl.ds(off, tile), sl], preferred_element_type=F32)
            new_accs.append(accs[hh] + pv)
            new_runs.append(runs[hh] + jnp.sum(lk, axis=1, keepdims=True))
        return tuple(new_accs), tuple(new_runs)

    acc0 = (jnp.zeros((tile, LANES), F32),) * n_heads
    run0 = (jnp.zeros((tile, 1), F32),) * n_heads
    accs, runs = block(qi, acc0, run0, True)

    def alive(runs):
        top = runs[0]
        for r in runs[1:]:
            top = jnp.maximum(top, r)
        return jnp.max(top) > SB_DEAD

    def cond(carry):
        return (carry[0] < qi) & carry[1]

    def body(carry):
        i, _, accs, runs = carry
        accs, runs = block(qi - 1 - i, accs, runs, False)
        return i + 1, alive(runs), accs, runs

    _, _, accs, runs = lax.while_loop(cond, body, (jnp.int32(0), alive(runs), accs, runs))
    for p in range(SB_PAIRS):
        o_ref[:, LANES * p:LANES * (p + 1)] = jnp.where(low, accs[2 * p], accs[2 * p + 1]).astype(BF16)


def _sb_attention(sb, batch, seq):
    tile = SB_TILE
    nq = seq // tile
    width = SB_PAIRS * LANES
    pairs = SB_W // width
    r = np.arange(tile)
    u = jnp.asarray(r[:, None] > r[None, :], BF16)
    return pl.pallas_call(
        functools.partial(_sb_kernel, tile=tile),
        grid=(batch, pairs, nq),
        in_specs=[pl.BlockSpec((tile, width), lambda b, p, i: (b * nq + i, p)),
                  pl.BlockSpec((seq, width), lambda b, p, i: (b, pairs + p)),
                  pl.BlockSpec((seq, width), lambda b, p, i: (b, 2 * pairs + p)),
                  pl.BlockSpec((tile, tile), lambda b, p, i: (0, 0))],
        out_specs=pl.BlockSpec((tile, width), lambda b, p, i: (b * nq + i, p)),
        out_shape=jax.ShapeDtypeStruct((batch * seq, SB_W), BF16),
        compiler_params=_cparams(("arbitrary",) * 3),
        name="sb_attention",
    )(sb, sb, sb, u)


def _compress_kernel(hb_ref, wc_ref, pe_ref, w1k_ref, w1v_ref, w2k_ref, w2v_ref, kg_ref, bd_ref,
                     kc_ref, vc_ref):
    p = jnp.dot(hb_ref[...], wc_ref[...], preferred_element_type=F32)
    n_half = p.shape[0]
    for kind in range(2):
        w1 = (w1k_ref, w1v_ref)[kind][...]
        w2 = (w2k_ref, w2v_ref)[kind][...]
        pe_term = jnp.dot(pe_ref[kind], w1, precision=HIGHEST, preferred_element_type=F32)[0:1, :]
        for g in range(NSA_KV_GROUPS):
            c = (kind * NSA_KV_GROUPS + g) * 2 * LANES
            a = p[:, c:c + LANES]
            b = p[:, c + LANES:c + 2 * LANES]
            pre = a + pltpu.roll(b, n_half - 1, 0) + pe_term
            hid = pre * jax.nn.sigmoid(pre)
            out = jnp.dot(hid.astype(BF16), w2, preferred_element_type=F32)
            if kind == 0:
                kc_ref[g] = _seg_rmsnorm(out, bd_ref[...], kg_ref[...]).astype(BF16)
            else:
                vc_ref[g] = out.astype(BF16)


def _compress_weights(w1k, w1v):
    cols = []
    for kind, w1 in enumerate((w1k, w1v)):
        w1r = w1.reshape(2, CMP_STRIDE, HEAD_DIM, CMP_HIDDEN)
        for g in range(NSA_KV_GROUPS):
            seg = kind * NSA_KV_GROUPS + g
            for half in range(2):
                full = jnp.zeros((CMP_STRIDE, 2 * NSA_KV_GROUPS, HEAD_DIM, CMP_HIDDEN), F32)
                full = full.at[:, seg].set(w1r[half])
                cols.append(full.reshape(CMP_STRIDE * 2 * NSA_KV_GROUPS * HEAD_DIM, CMP_HIDDEN))
    return jnp.concatenate(cols, axis=1).astype(BF16)


def _compress(kvc, batch, seq, pe_k, pe_v, w1k, w1v, w2k, w2v, k0g, bd):
    n_half = seq // CMP_STRIDE
    width = CMP_STRIDE * 2 * LANES
    hb = kvc.reshape(batch, n_half, width)
    wc = _compress_weights(w1k, w1v)
    fk = CMP_BLOCK * HEAD_DIM
    pe = jnp.stack([jnp.broadcast_to(pe_k.reshape(1, fk), (8, fk)),
                    jnp.broadcast_to(pe_v.reshape(1, fk), (8, fk))])
    dup = lambda w: jnp.concatenate([w, w], axis=1).astype(BF16)
    const = lambda shp: pl.BlockSpec(shp, lambda b: (0,) * len(shp))
    out_sds = jax.ShapeDtypeStruct((batch, NSA_KV_GROUPS, n_half, LANES), BF16)
    out_spec = pl.BlockSpec((None, NSA_KV_GROUPS, n_half, LANES), lambda b: (b, 0, 0, 0))
    return pl.pallas_call(
        _compress_kernel,
        grid=(batch,),
        in_specs=[pl.BlockSpec((None, n_half, width), lambda b: (b, 0, 0)),
                  const(wc.shape), const(pe.shape), const(w1k.shape), const(w1v.shape),
                  const((CMP_HIDDEN, LANES)), const((CMP_HIDDEN, LANES)), const((1, LANES)),
                  const((LANES, LANES))],
        out_specs=[out_spec, out_spec],
        out_shape=[out_sds, out_sds],
        compiler_params=_cparams(("arbitrary",)),
        name="nsa_compress",
    )(hb, wc, pe, w1k, w1v, dup(w2k), dup(w2v), k0g, bd)


def _alibi_slopes(n):
    return [float(v) for v in np.asarray(2.0 ** (-8.0 * np.arange(1, n + 1) / n), np.float32)]


def _nsa_kernel(q_ref, kc_ref, vc_ref, ks_ref, vs_ref, kw_ref, vw_ref, gt_ref, ov_ref, e_ref, o_ref,
                *, tq, tk, seq):
    g = pl.program_id(1)
    qi = pl.program_id(2)
    t0 = qi * tq
    lane = lax.broadcasted_iota(I32, (1, LANES), 1)
    low = lane < HEAD_DIM
    tpos = t0 + lax.broadcasted_iota(I32, (tq, 1), 0)
    slopes = _alibi_slopes(NSA_HEADS)
    slope = [jnp.where(g == 0, slopes[r], slopes[NSA_REP + r]) for r in range(NSA_REP)]

    q = q_ref[...]
    qz = []
    for r in range(NSA_REP):
        blk = q[:, LANES * (r // 2):LANES * (r // 2 + 1)]
        zero = jnp.zeros_like(blk)
        qz.append(jnp.where(low, blk, zero) if r % 2 == 0 else jnp.where(low, zero, blk))

    kc = kc_ref[...]
    vc = vc_ref[...]
    n_cmp = kc.shape[0]
    cend = lax.broadcasted_iota(I32, (1, n_cmp), 1) * CMP_STRIDE + (CMP_BLOCK - 1)
    mask_c = cend <= tpos
    dist_c = (tpos - cend).astype(F32)
    o_c = []
    psum = jnp.zeros((tq, n_cmp), F32)
    for r in range(NSA_REP):
        s = _dot_nt(qz[r], kc) - slope[r] * dist_c
        s = jnp.where(mask_c, s, -jnp.inf)
        m = jnp.max(s, axis=1, keepdims=True)
        m = jnp.where(m == -jnp.inf, 0.0, m)
        p = jnp.exp(s - m)
        p = p / jnp.maximum(jnp.sum(p, axis=1, keepdims=True), 1e-30)
        o_c.append(jnp.dot(p.astype(BF16), vc, preferred_element_type=F32))
        psum = psum + p

    n_slc = seq // SLC_BLOCK
    p_slc = _split_dot(psum, ov_ref[...], 2)
    cur = jnp.right_shift(tpos, int(math.log2(SLC_BLOCK)))
    valid = lane * SLC_BLOCK <= tpos
    forced = (lane == 0) | (lane == cur) | (lane == cur - 1)
    score = jnp.where(valid, jnp.where(forced, FORCE_SCORE, p_slc), -FORCE_SCORE)
    sc_t = score.T[0:n_slc, :]
    jrow = lax.broadcasted_iota(I32, (n_slc, 1), 0)
    cnt = jnp.zeros((n_slc, tq), F32)
    for i in range(n_slc):
        ri = sc_t[i:i + 1, :]
        cnt = cnt + jnp.where(jrow > i, jnp.where(ri >= sc_t, 1.0, 0.0), jnp.where(ri > sc_t, 1.0, 0.0))
    sel_t = jnp.where(cnt < float(min(SLC_TOPN, n_slc)), 1.0, 0.0)
    if n_slc < LANES:
        sel_t = jnp.concatenate([sel_t, jnp.zeros((LANES - n_slc, tq), F32)], axis=0)
    sel = sel_t.T.astype(BF16)

    def sel_body(kt, carry):
        ms, ls, accs = carry
        off = pl.multiple_of(kt * tk, tk)
        k = ks_ref[pl.ds(off, tk), :]
        v = vs_ref[pl.ds(off, tk), :]
        member = jnp.dot(sel, e_ref[:, pl.ds(off, tk)], preferred_element_type=F32)
        pos = off + lax.broadcasted_iota(I32, (1, tk), 1)
        ok = jnp.where(pos <= tpos, member, 0.0) > 0.5
        dist = (tpos - pos).astype(F32)
        new_ms, new_ls, new_accs = [], [], []
        for r in range(NSA_REP):
            s = _dot_nt(qz[r], k) - slope[r] * dist
            s = jnp.where(ok, s, NEG_BIG)
            m_new = jnp.maximum(ms[r], jnp.max(s, axis=1, keepdims=True))
            alpha = jnp.exp(ms[r] - m_new)
            p = jnp.exp(s - m_new)
            new_ls.append(alpha * ls[r] + jnp.sum(p, axis=1, keepdims=True))
            new_accs.append(alpha * accs[r] + jnp.dot(p.astype(BF16), v, preferred_element_type=F32))
            new_ms.append(m_new)
        return tuple(new_ms), tuple(new_ls), tuple(new_accs)

    n_kt = (t0 + tq + tk - 1) // tk
    init = ((jnp.full((tq, 1), NEG_BIG, F32),) * NSA_REP, (jnp.zeros((tq, 1), F32),) * NSA_REP,
            (jnp.zeros((tq, LANES), F32),) * NSA_REP)
    ms, ls, accs = lax.fori_loop(0, n_kt, sel_body, init)
    o_s = [accs[r] / ls[r] for r in range(NSA_REP)]

    span = WINDOW + tq
    start = pl.multiple_of(jnp.maximum(t0 - WINDOW, 0), tq)
    kw = kw_ref[pl.ds(start, span), :]
    vw = vw_ref[pl.ds(start, span), :]
    pos = start + lax.broadcasted_iota(I32, (1, span), 1)
    ok_w = jnp.where(pos <= tpos, pos, -WINDOW - 1) > tpos - WINDOW
    dist_w = (tpos - pos).astype(F32)
    o_w = []
    for r in range(NSA_REP):
        s = _dot_nt(qz[r], kw) - slope[r] * dist_w
        s = jnp.where(ok_w, s, -jnp.inf)
        m = jnp.max(s, axis=1, keepdims=True)
        p = jnp.exp(s - m)
        p = p / jnp.maximum(jnp.sum(p, axis=1, keepdims=True), 1e-30)
        o_w.append(jnp.dot(p.astype(BF16), vw, preferred_element_type=F32))

    gt = gt_ref[...]
    outs = []
    for r in range(NSA_REP):
        outs.append(gt[:, 3 * r:3 * r + 1] * o_c[r] + gt[:, 3 * r + 1:3 * r + 2] * o_s[r]
                    + gt[:, 3 * r + 2:3 * r + 3] * o_w[r])
    for j in range(NSA_REP // 2):
        o_ref[:, LANES * j:LANES * (j + 1)] = jnp.where(low, outs[2 * j], outs[2 * j + 1]).astype(BF16)


def _nsa_attention(nq, kcmp, vcmp, nkv, gates, batch, seq):
    tq, tk = NSA_TQ, NSA_TK
    nqt = seq // tq
    n_cmp_rows = kcmp.shape[2]
    n_slc = seq // SLC_BLOCK
    cmp_start = np.arange(n_cmp_rows) * CMP_STRIDE
    slc_start = np.arange(LANES) * SLC_BLOCK
    overlap = ((cmp_start[:, None] < slc_start[None, :] + SLC_BLOCK)
               & (cmp_start[:, None] + CMP_BLOCK > slc_start[None, :])
               & (np.arange(LANES)[None, :] < n_slc))
    ov = jnp.asarray(overlap, BF16)
    expand = jnp.asarray(np.arange(LANES)[:, None] == (np.arange(seq)[None, :] // SLC_BLOCK), BF16)
    gw = NSA_REP * HEAD_DIM
    kv_spec = lambda kind: pl.BlockSpec((seq, LANES), lambda b, g, i: (b, kind * NSA_KV_GROUPS + g))
    cmp_spec = pl.BlockSpec((None, None, n_cmp_rows, LANES), lambda b, g, i: (b, g, 0, 0))
    return pl.pallas_call(
        functools.partial(_nsa_kernel, tq=tq, tk=tk, seq=seq),
        grid=(batch, NSA_KV_GROUPS, nqt),
        in_specs=[pl.BlockSpec((tq, gw), lambda b, g, i: (b * nqt + i, g)),
                  cmp_spec, cmp_spec, kv_spec(0), kv_spec(1), kv_spec(2), kv_spec(3),
                  pl.BlockSpec((tq, LANES), lambda b, g, i: (b * nqt + i, g)),
                  pl.BlockSpec(ov.shape, lambda b, g, i: (0, 0)),
                  pl.BlockSpec(expand.shape, lambda b, g, i: (0, 0))],
        out_specs=pl.BlockSpec((tq, gw), lambda b, g, i: (b * nqt + i, g)),
        out_shape=jax.ShapeDtypeStruct((batch * seq, NSA_QW), BF16),
        compiler_params=_cparams(("arbitrary",) * 3),
        name="nsa_attention",
    )(nq, kcmp, vcmp, nkv, nkv, nkv, nkv, gates, ov, expand)


def _proj_residual_kernel(*refs, n_in):
    x_refs, w_refs = refs[:n_in], refs[n_in:2 * n_in]
    h_ref, g_ref, o_ref = refs[2 * n_in:]
    acc = None
    for x_ref, w_ref in zip(x_refs, w_refs):
        d = jnp.dot(x_ref[...], w_ref[...], preferred_element_type=F32)
        acc = d if acc is None else acc + d
    o_ref[...] = h_ref[...] + g_ref[...] * acc


def _proj_residual(xs, ws, h2, mod, layer, which, seq):
    t, d = h2.shape
    tm = ROW_TILE
    n_in = len(xs)
    row = lambda n: pl.BlockSpec((tm, n), lambda i: (i, 0))
    const = lambda shp: pl.BlockSpec(shp, lambda i: (0,) * len(shp))
    return pl.pallas_call(
        functools.partial(_proj_residual_kernel, n_in=n_in),
        grid=(t // tm,),
        in_specs=[row(x.shape[1]) for x in xs] + [const(w.shape) for w in ws] + [row(d)]
                 + _mod_specs(layer, (which,), seq // tm),
        out_specs=row(d),
        out_shape=jax.ShapeDtypeStruct((t, d), F32),
        compiler_params=_cparams(("arbitrary",)),
        name="proj_residual",
    )(*xs, *ws, h2, mod)


def _router_kernel(h_ref, gain_ref, sc_ref, sh_ref, w_ref, b_ref, tri_ref,
                   u_ref, e_ref, p_ref, r_ref, c_ref, cnt_ref):
    @pl.when(pl.program_id(0) == 0)
    def _():
        cnt_ref[...] = jnp.zeros_like(cnt_ref)

    u = _modulated_norm(h_ref[...], gain_ref[...], sc_ref[...], sh_ref[...])
    u_ref[...] = u.astype(BF16)
    logits = jnp.dot(u, w_ref[...], precision=HIGHEST, preferred_element_type=F32) + b_ref[...]
    tm = logits.shape[0]
    lane = lax.broadcasted_iota(I32, (tm, LANES), 1)
    lane_f = lane.astype(F32)
    work = logits
    tops, idxs, hots = [], [], []
    for _ in range(TOP_K):
        m = jnp.max(work, axis=1, keepdims=True)
        idx = jnp.min(jnp.where(work == m, lane_f, float(LANES)), axis=1, keepdims=True)
        hot = lane_f == idx
        work = jnp.where(hot, -jnp.inf, work)
        tops.append(m)
        idxs.append(idx)
        hots.append(hot)
    ex = [jnp.exp(m - tops[0]) for m in tops]
    den = ex[0] + ex[1] + ex[2] + ex[3]
    any_hot = jnp.zeros((tm, LANES), F32)
    for hot in hots:
        any_hot = any_hot + jnp.where(hot, 1.0, 0.0)
    before = jnp.dot(tri_ref[...], any_hot.astype(BF16), preferred_element_type=F32) + cnt_ref[...]
    e_out = jnp.zeros((tm, LANES), F32)
    p_out = jnp.zeros((tm, LANES), F32)
    r_out = jnp.zeros((tm, LANES), F32)
    for k in range(TOP_K):
        rank = jnp.sum(jnp.where(hots[k], before, 0.0), axis=1, keepdims=True)
        e_out = jnp.where(lane == k, idxs[k], e_out)
        p_out = jnp.where(lane == k, ex[k] / den, p_out)
        r_out = jnp.where(lane == k, rank, r_out)
    e_ref[...] = e_out.astype(I32)
    p_ref[...] = p_out
    r_ref[...] = r_out.astype(I32)
    cnt = cnt_ref[...] + jnp.sum(any_hot, axis=0, keepdims=True)
    cnt_ref[...] = cnt
    c_ref[...] = jnp.broadcast_to(cnt, c_ref.shape).astype(I32)


def _router(h2, mod, layer, gain, w_router, b_router, seq):
    t, d = h2.shape
    tm = ROW_TILE
    wp = jnp.pad(w_router, ((0, 0), (0, LANES - N_EXPERTS)))
    bp = jnp.pad(b_router, (0, LANES - N_EXPERTS), constant_values=NEG_BIG).reshape(1, LANES)
    r = np.arange(tm)
    tri = jnp.asarray(r[:, None] > r[None, :], BF16)
    row = lambda n: pl.BlockSpec((tm, n), lambda i: (i, 0))
    const = lambda shp: pl.BlockSpec(shp, lambda i: (0,) * len(shp))
    return pl.pallas_call(
        _router_kernel,
        grid=(t // tm,),
        in_specs=[row(d), const((1, d))] + _mod_specs(layer, (4, 3), seq // tm)
                 + [const((d, LANES)), const((1, LANES)), const((tm, tm))],
        out_specs=[row(d), row(LANES), row(LANES), row(LANES), const((8, LANES))],
        out_shape=[jax.ShapeDtypeStruct((t, d), BF16), jax.ShapeDtypeStruct((t, LANES), I32),
                   jax.ShapeDtypeStruct((t, LANES), F32), jax.ShapeDtypeStruct((t, LANES), I32),
                   jax.ShapeDtypeStruct((8, LANES), I32)],
        scratch_shapes=[pltpu.VMEM((1, LANES), F32)],
        compiler_params=_cparams(("arbitrary",)),
        name="moe_router",
    )(h2, gain, mod, mod, wp, bp, tri)


def _expert_kernel(be_ref, nu_ref, st_ref, u_hbm, wgu_ref, bgu_ref, wd_ref, bd_ref, o_ref,
                   wgu_bf, wd_bf, xbuf, sem):
    i = pl.program_id(0)
    tm = xbuf.shape[1]
    n_used = nu_ref[0]
    slot = lax.rem(i, 2)

    def row_copy(block, buf, r):
        tok = st_ref[block * tm + r]
        return pltpu.make_async_copy(u_hbm.at[pl.ds(tok, 1)], xbuf.at[buf, pl.ds(r, 1)], sem.at[buf])

    def start_rows(block, buf):
        def body(r, c):
            row_copy(block, buf, r).start()
            return c
        lax.fori_loop(0, tm, body, 0, unroll=8)

    def wait_rows(block, buf):
        def body(r, c):
            row_copy(block, buf, r).wait()
            return c
        lax.fori_loop(0, tm, body, 0, unroll=8)

    @pl.when((i == 0) & (n_used > 0))
    def _():
        start_rows(0, 0)

    @pl.when(i + 1 < n_used)
    def _():
        start_rows(i + 1, 1 - slot)

    prev = be_ref[jnp.maximum(i - 1, 0)]
    fresh = (i == 0) | (be_ref[i] != prev)

    @pl.when(fresh & (i < n_used))
    def _():
        rows = 128

        def cast_gu(j, c):
            off = pl.multiple_of(j * rows, rows)
            wgu_bf[pl.ds(off, rows), :] = wgu_ref[pl.ds(off, rows), :].astype(BF16)
            wd_bf[pl.ds(off, rows), :] = wd_ref[pl.ds(off, rows), :].astype(BF16)
            return c

        lax.fori_loop(0, D_MODEL // rows, cast_gu, 0)

    @pl.when(i < n_used)
    def _():
        wait_rows(i, slot)
        x = xbuf[slot].astype(BF16)
        gu = jnp.dot(x, wgu_bf[...], preferred_element_type=F32) + bgu_ref[...]
        gate = jnp.minimum(gu[:, :D_EXPERT], SWIGLU_LIMIT)
        up = jnp.clip(gu[:, D_EXPERT:], -SWIGLU_LIMIT, SWIGLU_LIMIT)
        act = (up + 1.0) * gate * jax.nn.sigmoid(SWIGLU_ALPHA * gate)
        y = jnp.dot(act.astype(BF16), wd_bf[...], preferred_element_type=F32) + bd_ref[...]
        o_ref[...] = y.astype(o_ref.dtype)

    @pl.when(i >= n_used)
    def _():
        o_ref[...] = jnp.zeros_like(o_ref)


def _experts(u, slot_tok, blk_e, n_used, layer, w_gu, b_gu, w_down, b_down):
    n_slots = slot_tok.shape[0]
    d = u.shape[1]
    tm = MOE_TILE
    _, ne, _, dgu = w_gu.shape
    grid_spec = pltpu.PrefetchScalarGridSpec(
        num_scalar_prefetch=3,
        grid=(n_slots // tm,),
        in_specs=[pl.BlockSpec(memory_space=pl.ANY),
                  pl.BlockSpec((None, None, d, dgu), lambda i, be, nu, st: (layer, be[i], 0, 0)),
                  pl.BlockSpec((None, 1, dgu), lambda i, be, nu, st: (be[i], 0, 0)),
                  pl.BlockSpec((None, None, D_EXPERT, d), lambda i, be, nu, st: (layer, be[i], 0, 0)),
                  pl.BlockSpec((None, 1, d), lambda i, be, nu, st: (be[i], 0, 0))],
        out_specs=pl.BlockSpec((tm, d), lambda i, be, nu, st: (i, 0)),
        scratch_shapes=[pltpu.VMEM((d, dgu), BF16), pltpu.VMEM((D_EXPERT, d), BF16),
                        pltpu.VMEM((2, tm, d), F32), pltpu.SemaphoreType.DMA((2,))],
    )
    return pl.pallas_call(
        _expert_kernel,
        grid_spec=grid_spec,
        out_shape=jax.ShapeDtypeStruct((n_slots, d), BF16),
        compiler_params=_cparams(("arbitrary",)),
        name="moe_experts",
    )(blk_e, n_used, slot_tok, u, w_gu, b_gu.reshape(ne, 1, dgu), w_down, b_down.reshape(ne, 1, d))


def _expert_rows_kernel(be_ref, nu_ref, x_ref, wgu_ref, bgu_ref, wd_ref, bd_ref, o_ref, wgu_bf, wd_bf):
    i = pl.program_id(0)
    n_used = nu_ref[0]
    prev = be_ref[jnp.maximum(i - 1, 0)]
    fresh = (i == 0) | (be_ref[i] != prev)

    @pl.when(fresh & (i < n_used))
    def _():
        rows = 128

        def cast_gu(j, c):
            off = pl.multiple_of(j * rows, rows)
            wgu_bf[pl.ds(off, rows), :] = wgu_ref[pl.ds(off, rows), :].astype(BF16)
            wd_bf[pl.ds(off, rows), :] = wd_ref[pl.ds(off, rows), :].astype(BF16)
            return c

        lax.fori_loop(0, D_MODEL // rows, cast_gu, 0)

    @pl.when(i < n_used)
    def _():
        gu = jnp.dot(x_ref[...], wgu_bf[...], preferred_element_type=F32) + bgu_ref[...]
        gate = jnp.minimum(gu[:, :D_EXPERT], SWIGLU_LIMIT)
        up = jnp.clip(gu[:, D_EXPERT:], -SWIGLU_LIMIT, SWIGLU_LIMIT)
        act = (up + 1.0) * gate * jax.nn.sigmoid(SWIGLU_ALPHA * gate)
        y = jnp.dot(act.astype(BF16), wd_bf[...], preferred_element_type=F32) + bd_ref[...]
        o_ref[...] = y.astype(o_ref.dtype)

    @pl.when(i >= n_used)
    def _():
        o_ref[...] = jnp.zeros_like(o_ref)


def _experts_rows(xs, blk_e, n_used, layer, w_gu, b_gu, w_down, b_down):
    n_slots, d = xs.shape
    tm = MOE_TILE
    _, ne, _, dgu = w_gu.shape
    grid_spec = pltpu.PrefetchScalarGridSpec(
        num_scalar_prefetch=2,
        grid=(n_slots // tm,),
        in_specs=[pl.BlockSpec((tm, d), lambda i, be, nu: (i, 0)),
                  pl.BlockSpec((None, None, d, dgu), lambda i, be, nu: (layer, be[i], 0, 0)),
                  pl.BlockSpec((None, 1, dgu), lambda i, be, nu: (be[i], 0, 0)),
                  pl.BlockSpec((None, None, D_EXPERT, d), lambda i, be, nu: (layer, be[i], 0, 0)),
                  pl.BlockSpec((None, 1, d), lambda i, be, nu: (be[i], 0, 0))],
        out_specs=pl.BlockSpec((tm, d), lambda i, be, nu: (i, 0)),
        scratch_shapes=[pltpu.VMEM((d, dgu), BF16), pltpu.VMEM((D_EXPERT, d), BF16)],
    )
    return pl.pallas_call(
        _expert_rows_kernel,
        grid_spec=grid_spec,
        out_shape=jax.ShapeDtypeStruct((n_slots, d), BF16),
        compiler_params=_cparams(("arbitrary",)),
        name="moe_experts",
    )(blk_e, n_used, xs, w_gu, b_gu.reshape(ne, 1, dgu), w_down, b_down.reshape(ne, 1, d))


def _combine_kernel(y_ref, p_ref, h_ref, g_ref, o_ref):
    p = p_ref[...]
    acc = p[:, 0:1] * y_ref[0].astype(F32)
    for k in range(1, TOP_K):
        acc = acc + p[:, k:k + 1] * y_ref[k].astype(F32)
    o_ref[...] = h_ref[...] + g_ref[...] * acc


def _combine(yg, top_p, h2, mod, layer, seq):
    t, d = h2.shape
    tm = ROW_TILE
    row = lambda n: pl.BlockSpec((tm, n), lambda i: (i, 0))
    return pl.pallas_call(
        _combine_kernel,
        grid=(t // tm,),
        in_specs=[pl.BlockSpec((TOP_K, tm, d), lambda i: (0, i, 0)), row(LANES), row(d)]
                 + _mod_specs(layer, (5,), seq // tm),
        out_specs=row(d),
        out_shape=jax.ShapeDtypeStruct((t, d), F32),
        compiler_params=_cparams(("arbitrary",)),
        name="moe_combine",
    )(yg, top_p, h2, mod)


def _moe(h2, mod, layer, gain, w_router, b_router, w_gu, b_gu, w_down, b_down, seq):
    t, d = h2.shape
    tm = MOE_TILE
    u, top_e, top_p, rank, counts = _router(h2, mod, layer, gain, w_router, b_router, seq)
    counts = counts[0, :N_EXPERTS]
    padded = (counts + tm - 1) // tm * tm
    pad_end = jnp.cumsum(padded)
    pad_start = pad_end - padded
    eids = jnp.arange(N_EXPERTS, dtype=I32)
    hot = top_e[:, :TOP_K, None] == eids
    slot = jnp.sum(jnp.where(hot, pad_start, 0), axis=-1) + rank[:, :TOP_K]
    n_slots = t * TOP_K + N_EXPERTS * tm
    tok = jnp.broadcast_to(jnp.arange(t, dtype=I32)[:, None], (t, TOP_K))
    slot_tok = jnp.zeros((n_slots,), I32).at[slot.reshape(-1)].set(tok.reshape(-1))
    n_blocks = n_slots // tm
    blk_row0 = jnp.arange(n_blocks, dtype=I32) * tm
    blk_e = jnp.minimum(jnp.sum((pad_end[None, :] <= blk_row0[:, None]).astype(I32), axis=1), N_EXPERTS - 1)
    n_used = (pad_end[-1] // tm).astype(I32).reshape(1)
    u_tab = jnp.pad(u, ((0, n_slots - t), (0, 0)))
    xs = u_tab.at[slot_tok].get(mode="promise_in_bounds")
    ys = _experts_rows(xs, blk_e, n_used, layer, w_gu, b_gu, w_down, b_down)
    yg = ys.at[slot.T.reshape(-1)].get(mode="promise_in_bounds").reshape(TOP_K, t, d)
    return _combine(yg, top_p, h2, mod, layer, seq)


SSM_Z0, SSM_X0, SSM_DT0, SSM_COLS = 0, SSM_INNER, SSM_INNER + SSM_CONV_DIM, SSM_INNER + SSM_CONV_DIM + SSM_GROUPS * LANES


def _ssm_weight_layout(w_in):
    parts = [w_in[:, :SSM_DT0]]
    for g in range(SSM_GROUPS):
        blk = w_in[:, SSM_DT0 + g * SSM_GROUP_HEADS: SSM_DT0 + (g + 1) * SSM_GROUP_HEADS]
        parts.append(jnp.pad(blk, ((0, 0), (0, LANES - SSM_GROUP_HEADS))))
    return jnp.concatenate(parts, axis=1).astype(BF16)


def _ssm_inproj_kernel(h_ref, gain_ref, sc_ref, sh_ref, w_ref, z_ref, xbc_ref, dt_ref):
    u = _modulated_norm(h_ref[...], gain_ref[...], sc_ref[...], sh_ref[...]).astype(BF16)
    step = 512
    for c in range(0, SSM_X0, step):
        z_ref[:, c:c + step] = jnp.dot(u, w_ref[:, c:c + step], preferred_element_type=F32)
    for c in range(0, SSM_CONV_DIM, step):
        xbc_ref[:, c:c + step] = jnp.dot(u, w_ref[:, SSM_X0 + c:SSM_X0 + c + step], preferred_element_type=F32)
    dt_ref[...] = jnp.dot(u, w_ref[:, SSM_DT0:SSM_COLS], preferred_element_type=F32)


def _ssm_inproj(h2, mod, layer, gain, w_bf, seq):
    t, d = h2.shape
    tm = ROW_TILE
    row = lambda n: pl.BlockSpec((tm, n), lambda i: (i, 0))
    const = lambda shp: pl.BlockSpec(shp, lambda i: (0,) * len(shp))
    return pl.pallas_call(
        _ssm_inproj_kernel,
        grid=(t // tm,),
        in_specs=[row(d), const((1, d))] + _mod_specs(layer, (1, 0), seq // tm) + [const((d, SSM_COLS))],
        out_specs=[row(SSM_INNER), row(SSM_CONV_DIM), row(SSM_GROUPS * LANES)],
        out_shape=[jax.ShapeDtypeStruct((t, SSM_INNER), F32), jax.ShapeDtypeStruct((t, SSM_CONV_DIM), F32),
                   jax.ShapeDtypeStruct((t, SSM_GROUPS * LANES), F32)],
        compiler_params=_cparams(("arbitrary",)),
        name="ssm_inproj",
    )(h2, gain, mod, mod, w_bf)


def _conv_kernel(x_ref, w_ref, b_ref, o_ref, ext_ref, *, ts):
    @pl.when(pl.program_id(2) == 0)
    def _():
        ext_ref[0:8, :] = jnp.zeros((8, ext_ref.shape[1]), F32)

    ext_ref[8:8 + ts, :] = x_ref[...]
    acc = b_ref[...] + w_ref[SSM_CONV - 1:SSM_CONV, :] * x_ref[...]
    for k in range(SSM_CONV - 1):
        acc = acc + w_ref[k:k + 1, :] * ext_ref[pl.ds(8 - (SSM_CONV - 1) + k, ts), :]
    o_ref[...] = acc * jax.nn.sigmoid(acc)
    ext_ref[0:8, :] = ext_ref[ts:ts + 8, :]


def _conv(xbc, conv_w, conv_b, batch, seq):
    t, c = xbc.shape
    ts, tc = CONV_ROWS, CONV_COLS
    ns = seq // ts
    return pl.pallas_call(
        functools.partial(_conv_kernel, ts=ts),
        grid=(c // tc, batch, ns),
        in_specs=[pl.BlockSpec((ts, tc), lambda j, b, i: (b * ns + i, j)),
                  pl.BlockSpec((SSM_CONV, tc), lambda j, b, i: (0, j)),
                  pl.BlockSpec((1, tc), lambda j, b, i: (0, j))],
        out_specs=pl.BlockSpec((ts, tc), lambda j, b, i: (b * ns + i, j)),
        out_shape=jax.ShapeDtypeStruct((t, c), F32),
        scratch_shapes=[pltpu.VMEM((ts + 8, tc), F32)],
        compiler_params=_cparams(("arbitrary",) * 3),
        name="ssm_conv",
    )(xbc, conv_w, conv_b.reshape(1, c))


def _ssd_kernel(x_ref, b_ref, c_ref, dt_ref, z_ref, dtb_ref, alog_ref, dsk_ref, ng_ref, tri_ref, ex_ref,
                o_ref, st_ref, *, chunk):
    @pl.when(pl.program_id(2) == 0)
    def _():
        st_ref[...] = jnp.zeros_like(st_ref)

    lane = lax.broadcasted_iota(I32, (1, LANES), 1)
    low = lane < SSM_HEAD_DIM
    x = x_ref[...]
    dt = _softplus(dt_ref[...] + dtb_ref[...])
    a = -jnp.exp(alog_ref[...])
    cs = _split_dot_left(tri_ref[...], dt * a, 3)
    ex = ex_ref[...]
    cs_x = _split_dot(cs, ex, 3)
    dt_x = _split_dot(dt, ex, 3)
    last_x = cs_x[chunk - 1:chunk, :]
    xd = x * dt_x
    bm = b_ref[...]
    cm = c_ref[...].astype(BF16)
    bt = bm.T.astype(BF16)
    cb = jnp.dot(cm, bt, preferred_element_type=F32)
    cs_t = cs.T
    row = lax.broadcasted_iota(I32, (chunk, chunk), 0)
    col = lax.broadcasted_iota(I32, (chunk, chunk), 1)
    causal = col <= row
    decay_in = jnp.exp(cs_x)
    w_end = jnp.exp(last_x - cs_x)
    decay_chunk = jnp.exp(last_x)
    ys = []
    for jp in range(SSM_GROUP_HEADS // 2):
        sl = slice(LANES * jp, LANES * (jp + 1))
        xd_pair = xd[:, sl]
        xd_bf = xd_pair.astype(BF16)
        yd = []
        for hh in range(2):
            j = 2 * jp + hh
            seg = cs[:, j:j + 1] - cs_t[j:j + 1, :]
            lm = jnp.exp(jnp.where(causal, seg, -jnp.inf))
            yd.append(jnp.dot((cb * lm).astype(BF16), xd_bf, preferred_element_type=F32))
        y = jnp.where(low, yd[0], yd[1])
        prev = st_ref[jp]
        y = y + jnp.dot(cm, prev.astype(BF16), preferred_element_type=F32) * decay_in[:, sl]
        new = jnp.dot(bt, (xd_pair * w_end[:, sl]).astype(BF16), preferred_element_type=F32)
        st_ref[jp] = prev * decay_chunk[:, sl] + new
        ys.append(y + x[:, sl] * dsk_ref[:, sl])
    y = jnp.concatenate(ys, axis=1)
    z = z_ref[...]
    y = y * (z * jax.nn.sigmoid(z))
    y = y * lax.rsqrt(jnp.mean(y * y, axis=-1, keepdims=True) + EPS)
    o_ref[...] = (y * ng_ref[...]).astype(BF16)


def _ssd(xbc, dt, z, dt_bias, a_log, d_skip, norm_g, batch, seq):
    t = xbc.shape[0]
    chunk = math.gcd(seq, SSM_CHUNK)
    nc = seq // chunk
    gw = SSM_GROUP_W
    xblocks = SSM_INNER // gw
    bblk0 = SSM_INNER // SSM_STATE
    cblk0 = bblk0 + SSM_GROUPS
    pad_g = lambda v: jnp.pad(v.reshape(SSM_GROUPS, 1, SSM_GROUP_HEADS),
                              ((0, 0), (0, 0), (0, LANES - SSM_GROUP_HEADS)))
    dsk = jnp.repeat(d_skip, SSM_HEAD_DIM).reshape(1, SSM_INNER)
    r = np.arange(chunk)
    tri = jnp.asarray(r[:, None] >= r[None, :], BF16)
    ex = jnp.asarray(np.arange(LANES)[:, None] == (np.arange(gw)[None, :] // SSM_HEAD_DIM), BF16)
    rowblk = lambda n, colfn: pl.BlockSpec((chunk, n), lambda b, g, c: (b * nc + c, colfn(g)))
    grp = pl.BlockSpec((None, 1, LANES), lambda b, g, c: (g, 0, 0))
    return pl.pallas_call(
        functools.partial(_ssd_kernel, chunk=chunk),
        grid=(batch, SSM_GROUPS, nc),
        in_specs=[rowblk(gw, lambda g: g), rowblk(SSM_STATE, lambda g: bblk0 + g),
                  rowblk(SSM_STATE, lambda g: cblk0 + g), rowblk(LANES, lambda g: g), rowblk(gw, lambda g: g),
                  grp, grp,
                  pl.BlockSpec((1, gw), lambda b, g, c: (0, g)), pl.BlockSpec((1, gw), lambda b, g, c: (0, g)),
                  pl.BlockSpec((chunk, chunk), lambda b, g, c: (0, 0)),
                  pl.BlockSpec((LANES, gw), lambda b, g, c: (0, 0))],
        out_specs=rowblk(gw, lambda g: g),
        out_shape=jax.ShapeDtypeStruct((t, SSM_INNER), BF16),
        scratch_shapes=[pltpu.VMEM((SSM_GROUP_HEADS // 2, SSM_STATE, LANES), F32)],
        compiler_params=_cparams(("arbitrary",) * 3),
        name="ssd",
    )(xbc, xbc, xbc, dt, z, pad_g(dt_bias), pad_g(a_log), dsk, norm_g.reshape(1, SSM_INNER), tri, ex)


def _attention_layer(h2, mod, layer, norm_mix, w_in, w_out, q_norm, k_norm, pe_k, pe_v, w1k, w2k, w1v, w2v,
                     batch, seq):
    tile2 = lambda v: jnp.concatenate([v, v]).reshape(1, LANES)
    bd = jnp.asarray(np.arange(LANES)[:, None] // HEAD_DIM == np.arange(LANES)[None, :] // HEAD_DIM, BF16)
    sb, nq, kvc, nkv, gates = _attn_inproj(h2, mod, layer, norm_mix.reshape(1, -1), _attn_weight_layout(w_in),
                                           tile2(q_norm), tile2(k_norm[1]), tile2(k_norm[2]), bd, seq)
    o_sb = _sb_attention(sb, batch, seq)
    kcmp, vcmp = _compress(kvc, batch, seq, pe_k, pe_v, w1k, w1v, w2k, w2v, tile2(k_norm[0]), bd)
    o_nsa = _nsa_attention(nq, kcmp, vcmp, nkv, gates, batch, seq)
    w_out_bf = w_out.astype(BF16)
    return _proj_residual([o_sb, o_nsa], [w_out_bf[:SB_W], w_out_bf[SB_W:]], h2, mod, layer, 2, seq)


def _mamba_layer(h2, mod, layer, norm_mix, w_in, conv_w, conv_b, dt_bias, a_log, d_skip, norm_g, w_out,
                 batch, seq):
    z, xbc, dt = _ssm_inproj(h2, mod, layer, norm_mix.reshape(1, -1), _ssm_weight_layout(w_in), seq)
    xbc = _conv(xbc, conv_w, conv_b, batch, seq)
    y = _ssd(xbc, dt, z, dt_bias, a_log, d_skip, norm_g, batch, seq)
    return _proj_residual([y], [w_out.astype(BF16)], h2, mod, layer, 2, seq)


def kernel(x, c, ada_w, ada_b, norm_mix, norm_ffn, attn_w_in, attn_w_out, nsa_q_norm, nsa_k_norm, cmp_pe_k, cmp_pe_v, cmp_w1_k, cmp_w2_k, cmp_w1_v, cmp_w2_v, ssm_w_in, ssm_conv_w, ssm_conv_b, ssm_dt_bias, ssm_a_log, ssm_d, ssm_norm, ssm_w_out, router_w, router_b, moe_w_gu, moe_b_gu, moe_w_down, moe_b_down):
    batch, seq, d = x.shape
    depth = ada_w.shape[0]
    c_pad = jnp.pad(c, ((0, 8 - batch % 8 if batch % 8 else 0), (0, 0)))
    mod = _adaln(c_pad, ada_w, ada_b)
    mod = mod.reshape(depth, c_pad.shape[0], 6, 1, d)
    h2 = x.reshape(batch * seq, d)
    for layer in range(depth):
        i = layer // 2
        if layer % 2 == 0:
            h2 = _attention_layer(h2, mod, layer, norm_mix[layer], attn_w_in[i], attn_w_out[i], nsa_q_norm[i],
                                  nsa_k_norm[i], cmp_pe_k[i], cmp_pe_v[i], cmp_w1_k[i], cmp_w2_k[i],
                                  cmp_w1_v[i], cmp_w2_v[i], batch, seq)
        else:
            h2 = _mamba_layer(h2, mod, layer, norm_mix[layer], ssm_w_in[i], ssm_conv_w[i], ssm_conv_b[i],
                              ssm_dt_bias[i], ssm_a_log[i], ssm_d[i], ssm_norm[i], ssm_w_out[i], batch, seq)
        h2 = _moe(h2, mod, layer, norm_ffn[layer].reshape(1, -1), router_w[layer], router_b[layer],
                  moe_w_gu, moe_b_gu[layer], moe_w_down, moe_b_down[layer], seq)
    return h2.reshape(batch, seq, d)
```

```python
import functools
import math

import numpy as np
import jax
import jax.numpy as jnp
from jax import lax
from jax.experimental import pallas as pl
from jax.experimental.pallas import tpu as pltpu

F32 = jnp.float32
BF16 = jnp.bfloat16
I32 = jnp.int32
HIGHEST = lax.Precision.HIGHEST

D_MODEL = 1024
HEAD_DIM = 64
LANES = 128
SB_HEADS = 8
NSA_HEADS = 8
NSA_KV_GROUPS = 2
NSA_REP = NSA_HEADS // NSA_KV_GROUPS
CMP_BLOCK = 32
CMP_STRIDE = 16
CMP_HIDDEN = 128
SLC_BLOCK = 64
SLC_TOPN = 16
WINDOW = 512
FORCE_SCORE = 1e6
SB_W = SB_HEADS * HEAD_DIM
NSA_QW = NSA_HEADS * HEAD_DIM
SSM_INNER = 2 * D_MODEL
SSM_HEAD_DIM = 64
SSM_HEADS = SSM_INNER // SSM_HEAD_DIM
SSM_GROUPS = 4
SSM_GROUP_HEADS = SSM_HEADS // SSM_GROUPS
SSM_GROUP_W = SSM_INNER // SSM_GROUPS
SSM_STATE = 128
SSM_CONV = 4
SSM_CHUNK = 256
SSM_CONV_DIM = SSM_INNER + 2 * SSM_GROUPS * SSM_STATE
N_EXPERTS = 32
TOP_K = 4
D_EXPERT = D_MODEL
SWIGLU_LIMIT = 7.0
SWIGLU_ALPHA = 1.702
EPS = 1e-6
NEG_BIG = -1e30
SB_DEAD = -110.0

VMEM_LIMIT = 56 * 1024 * 1024

ROW_TILE = 256
SB_TILE = 256
SB_PAIRS = 2
NSA_TQ = 256
NSA_TK = 512
MOE_TILE = 256
CONV_ROWS = 512
CONV_COLS = 1024


def _cparams(sem):
    return pltpu.CompilerParams(dimension_semantics=sem, vmem_limit_bytes=VMEM_LIMIT)


def _split_dot(a, m01, terms):
    out = None
    r = a
    for t in range(terms):
        hi = r.astype(BF16)
        d = jnp.dot(hi, m01, preferred_element_type=F32)
        out = d if out is None else out + d
        if t + 1 < terms:
            r = r - hi.astype(F32)
    return out


def _split_dot_left(m01, a, terms):
    out = None
    r = a
    for t in range(terms):
        hi = r.astype(BF16)
        d = jnp.dot(m01, hi, preferred_element_type=F32)
        out = d if out is None else out + d
        if t + 1 < terms:
            r = r - hi.astype(F32)
    return out


def _dot_nt(a, b):
    return lax.dot_general(a, b, (((1,), (1,)), ((), ())), preferred_element_type=F32)


def _softplus(x):
    return jnp.maximum(x, 0.0) + jnp.log(1.0 + jnp.exp(-jnp.abs(x)))


def _modulated_norm(h, gain, sc, sh):
    ms = jnp.mean(h * h, axis=-1, keepdims=True)
    y = h * lax.rsqrt(ms + EPS) * gain
    return y * (1.0 + sc) + sh


def _seg_rmsnorm(x, bd, gain):
    ssq = _split_dot(x * x, bd, 2)
    return x * lax.rsqrt(ssq * (1.0 / HEAD_DIM) + EPS) * gain


def _mod_specs(layer, whichs, tiles_per_batch):
    def mk(which):
        return pl.BlockSpec((None, None, None, 1, D_MODEL),
                            lambda i, *_: (layer, i // tiles_per_batch, which, 0, 0))
    return [mk(w) for w in whichs]


def _adaln_kernel(c_ref, w_ref, b_ref, o_ref):
    c = c_ref[...]
    s = c * jax.nn.sigmoid(c)
    o_ref[...] = jnp.dot(s, w_ref[...], precision=HIGHEST, preferred_element_type=F32) + b_ref[...]


def _adaln(c_pad, ada_w, ada_b):
    depth, d, n = ada_w.shape
    bp = c_pad.shape[0]
    tn = 1536
    return pl.pallas_call(
        _adaln_kernel,
        grid=(depth, n // tn),
        in_specs=[pl.BlockSpec((bp, d), lambda l, j: (0, 0)),
                  pl.BlockSpec((None, d, tn), lambda l, j: (l, 0, j)),
                  pl.BlockSpec((None, 1, tn), lambda l, j: (l, 0, j))],
        out_specs=pl.BlockSpec((None, bp, tn), lambda l, j: (l, 0, j)),
        out_shape=jax.ShapeDtypeStruct((depth, bp, n), F32),
        compiler_params=_cparams(("arbitrary", "arbitrary")),
        name="adaln",
    )(c_pad, ada_w, ada_b.reshape(depth, 1, n))


ATTN_SB0, ATTN_NQ0, ATTN_KVC0, ATTN_NKV0, ATTN_GT0, ATTN_COLS = 0, 1536, 2048, 2304, 3328, 3584


def _attn_weight_layout(w_in):
    d = w_in.shape[0]
    off = 3 * SB_W + NSA_QW
    gw = NSA_KV_GROUPS * HEAD_DIM
    parts = [w_in[:, :off + 2 * gw]]
    for kind in range(4):
        base = off + 2 * gw + kind * gw
        for g in range(NSA_KV_GROUPS):
            blk = w_in[:, base + g * HEAD_DIM: base + (g + 1) * HEAD_DIM]
            parts += [blk, blk]
    gbase = off + 6 * gw
    for g in range(NSA_KV_GROUPS):
        blk = w_in[:, gbase + g * NSA_REP * 3: gbase + (g + 1) * NSA_REP * 3]
        parts.append(jnp.pad(blk, ((0, 0), (0, LANES - NSA_REP * 3))))
    w = jnp.concatenate(parts, axis=1)
    assert w.shape == (d, ATTN_COLS)
    return w.astype(BF16)


def _attn_inproj_kernel(h_ref, gain_ref, sc_ref, sh_ref, w_ref, qg_ref, k1g_ref, k2g_ref, bd_ref,
                        sb_ref, nq_ref, kvc_ref, nkv_ref, gt_ref):
    u = _modulated_norm(h_ref[...], gain_ref[...], sc_ref[...], sh_ref[...]).astype(BF16)
    bd = bd_ref[...]
    scale = HEAD_DIM ** -0.5

    def proj(a, b):
        return jnp.dot(u, w_ref[:, a:b], preferred_element_type=F32)

    sb_ref[:, 0:SB_W] = (proj(0, SB_W) * scale).astype(BF16)
    sb_ref[:, SB_W:3 * SB_W] = proj(SB_W, 3 * SB_W).astype(BF16)
    for j in range(NSA_QW // LANES):
        x = proj(ATTN_NQ0 + LANES * j, ATTN_NQ0 + LANES * (j + 1))
        nq_ref[:, LANES * j:LANES * (j + 1)] = (_seg_rmsnorm(x, bd, qg_ref[...]) * scale).astype(BF16)
    kvc_ref[...] = proj(ATTN_KVC0, ATTN_NKV0).astype(BF16)
    for kind in range(4):
        for g in range(NSA_KV_GROUPS):
            c = (kind * NSA_KV_GROUPS + g) * LANES
            x = proj(ATTN_NKV0 + c, ATTN_NKV0 + c + LANES)
            if kind == 0:
                x = _seg_rmsnorm(x, bd, k1g_ref[...])
            elif kind == 2:
                x = _seg_rmsnorm(x, bd, k2g_ref[...])
            nkv_ref[:, c:c + LANES] = x.astype(BF16)
    gt_ref[...] = jax.nn.sigmoid(proj(ATTN_GT0, ATTN_COLS))


def _attn_inproj(h2, mod, layer, gain, w_bf, qg, k1g, k2g, bd, seq):
    t, d = h2.shape
    tm = ROW_TILE
    tpb = seq // tm
    row = lambda n: pl.BlockSpec((tm, n), lambda i: (i, 0))
    const = lambda shp: pl.BlockSpec(shp, lambda i: (0,) * len(shp))
    return pl.pallas_call(
        _attn_inproj_kernel,
        grid=(t // tm,),
        in_specs=[row(d), const((1, d))] + _mod_specs(layer, (1, 0), tpb)
                 + [const((d, ATTN_COLS)), const((1, LANES)), const((1, LANES)), const((1, LANES)),
                    const((LANES, LANES))],
        out_specs=[row(3 * SB_W), row(NSA_QW), row(2 * LANES), row(8 * LANES), row(2 * LANES)],
        out_shape=[jax.ShapeDtypeStruct((t, 3 * SB_W), BF16), jax.ShapeDtypeStruct((t, NSA_QW), BF16),
                   jax.ShapeDtypeStruct((t, 2 * LANES), BF16), jax.ShapeDtypeStruct((t, 8 * LANES), BF16),
                   jax.ShapeDtypeStruct((t, 2 * LANES), F32)],
        compiler_params=_cparams(("arbitrary",)),
        name="attn_inproj",
    )(h2, gain, mod, mod, w_bf, qg, k1g, k2g, bd)


def _sb_kernel(q_ref, k_ref, v_ref, u_ref, o_ref, *, tile):
    qi = pl.program_id(2)
    lane = lax.broadcasted_iota(I32, (1, LANES), 1)
    low = lane < HEAD_DIM
    n_heads = 2 * SB_PAIRS
    qs = []
    for p in range(SB_PAIRS):
        q = q_ref[:, LANES * p:LANES * (p + 1)]
        zero = jnp.zeros_like(q)
        qs += [jnp.where(low, q, zero), jnp.where(low, zero, q)]
    u = u_ref[...]
    row = lax.broadcasted_iota(I32, (tile, tile), 0)
    col = lax.broadcasted_iota(I32, (tile, tile), 1)
    causal = col < row

    def block(kb, accs, runs, masked):
        off = pl.multiple_of(kb * tile, tile)
        new_accs, new_runs = [], []
        for hh in range(n_heads):
            sl = slice(LANES * (hh // 2), LANES * (hh // 2 + 1))
            x = _dot_nt(qs[hh], k_ref[pl.ds(off, tile), sl])
            sp = _softplus(x)
            lk = -sp
            if masked:
                lk = jnp.where(causal, lk, 0.0)
            tot = _split_dot(lk, u, 2) + runs[hh]
            w = jnp.exp(x - sp + tot)
            if masked:
                w = jnp.where(causal, w, 0.0)
            pv = jnp.dot(w.astype(BF16), v_ref[pl.ds(off, tile), sl], preferred_element_type=F32)
            new_accs.append(accs[hh] + pv)
            new_runs.append(runs[hh] + jnp.sum(lk, axis=1, keepdims=True))
        return tuple(new_accs), tuple(new_runs)

    acc0 = (jnp.zeros((tile, LANES), F32),) * n_heads
    run0 = (jnp.zeros((tile, 1), F32),) * n_heads
    accs, runs = block(qi, acc0, run0, True)

    def alive(runs):
        top = runs[0]
        for r in runs[1:]:
            top = jnp.maximum(top, r)
        return jnp.max(top) > SB_DEAD

    def cond(carry):
        return (carry[0] < qi) & carry[1]

    def body(carry):
        i, _, accs, runs = carry
        accs, runs = block(qi - 1 - i, accs, runs, False)
        return i + 1, alive(runs), accs, runs

    _, _, accs, runs = lax.while_loop(cond, body, (jnp.int32(0), alive(runs), accs, runs))
    for p in range(SB_PAIRS):
        o_ref[:, LANES * p:LANES * (p + 1)] = jnp.where(low, accs[2 * p], accs[2 * p + 1]).astype(BF16)


def _sb_attention(sb, batch, seq):
    tile = SB_TILE
    nq = seq // tile
    width = SB_PAIRS * LANES
    pairs = SB_W // width
    r = np.arange(tile)
    u = jnp.asarray(r[:, None] > r[None, :], BF16)
    return pl.pallas_call(
        functools.partial(_sb_kernel, tile=tile),
        grid=(batch, pairs, nq),
        in_specs=[pl.BlockSpec((tile, width), lambda b, p, i: (b * nq + i, p)),
                  pl.BlockSpec((seq, width), lambda b, p, i: (b, pairs + p)),
                  pl.BlockSpec((seq, width), lambda b, p, i: (b, 2 * pairs + p)),
                  pl.BlockSpec((tile, tile), lambda b, p, i: (0, 0))],
        out_specs=pl.BlockSpec((tile, width), lambda b, p, i: (b * nq + i, p)),
        out_shape=jax.ShapeDtypeStruct((batch * seq, SB_W), BF16),
        compiler_params=_cparams(("arbitrary",) * 3),
        name="sb_attention",
    )(sb, sb, sb, u)


def _compress_kernel(hb_ref, wc_ref, pe_ref, w1k_ref, w1v_ref, w2k_ref, w2v_ref, kg_ref, bd_ref,
                     kc_ref, vc_ref):
    p = jnp.dot(hb_ref[...], wc_ref[...], preferred_element_type=F32)
    n_half = p.shape[0]
    for kind in range(2):
        w1 = (w1k_ref, w1v_ref)[kind][...]
        w2 = (w2k_ref, w2v_ref)[kind][...]
        pe_term = jnp.dot(pe_ref[kind], w1, precision=HIGHEST, preferred_element_type=F32)[0:1, :]
        for g in range(NSA_KV_GROUPS):
            c = (kind * NSA_KV_GROUPS + g) * 2 * LANES
            a = p[:, c:c + LANES]
            b = p[:, c + LANES:c + 2 * LANES]
            pre = a + pltpu.roll(b, n_half - 1, 0) + pe_term
            hid = pre * jax.nn.sigmoid(pre)
            out = jnp.dot(hid.astype(BF16), w2, preferred_element_type=F32)
            if kind == 0:
                kc_ref[g] = _seg_rmsnorm(out, bd_ref[...], kg_ref[...]).astype(BF16)
            else:
                vc_ref[g] = out.astype(BF16)


def _compress_weights(w1k, w1v):
    cols = []
    for kind, w1 in enumerate((w1k, w1v)):
        w1r = w1.reshape(2, CMP_STRIDE, HEAD_DIM, CMP_HIDDEN)
        for g in range(NSA_KV_GROUPS):
            seg = kind * NSA_KV_GROUPS + g
            for half in range(2):
                full = jnp.zeros((CMP_STRIDE, 2 * NSA_KV_GROUPS, HEAD_DIM, CMP_HIDDEN), F32)
                full = full.at[:, seg].set(w1r[half])
                cols.append(full.reshape(CMP_STRIDE * 2 * NSA_KV_GROUPS * HEAD_DIM, CMP_HIDDEN))
    return jnp.concatenate(cols, axis=1).astype(BF16)


def _compress(kvc, batch, seq, pe_k, pe_v, w1k, w1v, w2k, w2v, k0g, bd):
    n_half = seq // CMP_STRIDE
    width = CMP_STRIDE * 2 * LANES
    hb = kvc.reshape(batch, n_half, width)
    wc = _compress_weights(w1k, w1v)
    fk = CMP_BLOCK * HEAD_DIM
    pe = jnp.stack([jnp.broadcast_to(pe_k.reshape(1, fk), (8, fk)),
                    jnp.broadcast_to(pe_v.reshape(1, fk), (8, fk))])
    dup = lambda w: jnp.concatenate([w, w], axis=1).astype(BF16)
    const = lambda shp: pl.BlockSpec(shp, lambda b: (0,) * len(shp))
    out_sds = jax.ShapeDtypeStruct((batch, NSA_KV_GROUPS, n_half, LANES), BF16)
    out_spec = pl.BlockSpec((None, NSA_KV_GROUPS, n_half, LANES), lambda b: (b, 0, 0, 0))
    return pl.pallas_call(
        _compress_kernel,
        grid=(batch,),
        in_specs=[pl.BlockSpec((None, n_half, width), lambda b: (b, 0, 0)),
                  const(wc.shape), const(pe.shape), const(w1k.shape), const(w1v.shape),
                  const((CMP_HIDDEN, LANES)), const((CMP_HIDDEN, LANES)), const((1, LANES)),
                  const((LANES, LANES))],
        out_specs=[out_spec, out_spec],
        out_shape=[out_sds, out_sds],
        compiler_params=_cparams(("arbitrary",)),
        name="nsa_compress",
    )(hb, wc, pe, w1k, w1v, dup(w2k), dup(w2v), k0g, bd)


def _alibi_slopes(n):
    return [float(v) for v in np.asarray(2.0 ** (-8.0 * np.arange(1, n + 1) / n), np.float32)]


def _nsa_kernel(q_ref, kc_ref, vc_ref, ks_ref, vs_ref, kw_ref, vw_ref, gt_ref, ov_ref, e_ref, o_ref,
                *, tq, tk, seq):
    g = pl.program_id(1)
    qi = pl.program_id(2)
    t0 = qi * tq
    lane = lax.broadcasted_iota(I32, (1, LANES), 1)
    low = lane < HEAD_DIM
    tpos = t0 + lax.broadcasted_iota(I32, (tq, 1), 0)
    slopes = _alibi_slopes(NSA_HEADS)
    slope = [jnp.where(g == 0, slopes[r], slopes[NSA_REP + r]) for r in range(NSA_REP)]

    q = q_ref[...]
    qz = []
    for r in range(NSA_REP):
        blk = q[:, LANES * (r // 2):LANES * (r // 2 + 1)]
        zero = jnp.zeros_like(blk)
        qz.append(jnp.where(low, blk, zero) if r % 2 == 0 else jnp.where(low, zero, blk))

    kc = kc_ref[...]
    vc = vc_ref[...]
    n_cmp = kc.shape[0]
    cend = lax.broadcasted_iota(I32, (1, n_cmp), 1) * CMP_STRIDE + (CMP_BLOCK - 1)
    mask_c = cend <= tpos
    dist_c = (tpos - cend).astype(F32)
    o_c = []
    psum = jnp.zeros((tq, n_cmp), F32)
    for r in range(NSA_REP):
        s = _dot_nt(qz[r], kc) - slope[r] * dist_c
        s = jnp.where(mask_c, s, -jnp.inf)
        m = jnp.max(s, axis=1, keepdims=True)
        m = jnp.where(m == -jnp.inf, 0.0, m)
        p = jnp.exp(s - m)
        p = p / jnp.maximum(jnp.sum(p, axis=1, keepdims=True), 1e-30)
        o_c.append(jnp.dot(p.astype(BF16), vc, preferred_element_type=F32))
        psum = psum + p

    n_slc = seq // SLC_BLOCK
    p_slc = _split_dot(psum, ov_ref[...], 2)
    cur = jnp.right_shift(tpos, int(math.log2(SLC_BLOCK)))
    valid = lane * SLC_BLOCK <= tpos
    forced = (lane == 0) | (lane == cur) | (lane == cur - 1)
    score = jnp.where(valid, jnp.where(forced, FORCE_SCORE, p_slc), -FORCE_SCORE)
    sc_t = score.T[0:n_slc, :]
    jrow = lax.broadcasted_iota(I32, (n_slc, 1), 0)
    cnt = jnp.zeros((n_slc, tq), F32)
    for i in range(n_slc):
        ri = sc_t[i:i + 1, :]
        cnt = cnt + jnp.where(jrow > i, jnp.where(ri >= sc_t, 1.0, 0.0), jnp.where(ri > sc_t, 1.0, 0.0))
    sel_t = jnp.where(cnt < float(min(SLC_TOPN, n_slc)), 1.0, 0.0)
    if n_slc < LANES:
        sel_t = jnp.concatenate([sel_t, jnp.zeros((LANES - n_slc, tq), F32)], axis=0)
    sel = sel_t.T.astype(BF16)

    def sel_body(kt, carry):
        ms, ls, accs = carry
        off = pl.multiple_of(kt * tk, tk)
        k = ks_ref[pl.ds(off, tk), :]
        v = vs_ref[pl.ds(off, tk), :]
        member = jnp.dot(sel, e_ref[:, pl.ds(off, tk)], preferred_element_type=F32)
        pos = off + lax.broadcasted_iota(I32, (1, tk), 1)
        ok = jnp.where(pos <= tpos, member, 0.0) > 0.5
        dist = (tpos - pos).astype(F32)
        new_ms, new_ls, new_accs = [], [], []
        for r in range(NSA_REP):
            s = _dot_nt(qz[r], k) - slope[r] * dist
            s = jnp.where(ok, s, NEG_BIG)
            m_new = jnp.maximum(ms[r], jnp.max(s, axis=1, keepdims=True))
            alpha = jnp.exp(ms[r] - m_new)
            p = jnp.exp(s - m_new)
            new_ls.append(alpha * ls[r] + jnp.sum(p, axis=1, keepdims=True))
            new_accs.append(alpha * accs[r] + jnp.dot(p.astype(BF16), v, preferred_element_type=F32))
            new_ms.append(m_new)
        return tuple(new_ms), tuple(new_ls), tuple(new_accs)

    n_kt = (t0 + tq + tk - 1) // tk
    init = ((jnp.full((tq, 1), NEG_BIG, F32),) * NSA_REP, (jnp.zeros((tq, 1), F32),) * NSA_REP,
            (jnp.zeros((tq, LANES), F32),) * NSA_REP)
    ms, ls, accs = lax.fori_loop(0, n_kt, sel_body, init)
    o_s = [accs[r] / ls[r] for r in range(NSA_REP)]

    span = WINDOW + tq
    start = pl.multiple_of(jnp.maximum(t0 - WINDOW, 0), tq)
    kw = kw_ref[pl.ds(start, span), :]
    vw = vw_ref[pl.ds(start, span), :]
    pos = start + lax.broadcasted_iota(I32, (1, span), 1)
    ok_w = jnp.where(pos <= tpos, pos, -WINDOW - 1) > tpos - WINDOW
    dist_w = (tpos - pos).astype(F32)
    o_w = []
    for r in range(NSA_REP):
        s = _dot_nt(qz[r], kw) - slope[r] * dist_w
        s = jnp.where(ok_w, s, -jnp.inf)
        m = jnp.max(s, axis=1, keepdims=True)
        p = jnp.exp(s - m)
        p = p / jnp.maximum(jnp.sum(p, axis=1, keepdims=True), 1e-30)
        o_w.append(jnp.dot(p.astype(BF16), vw, preferred_element_type=F32))

    gt = gt_ref[...]
    outs = []
    for r in range(NSA_REP):
        outs.append(gt[:, 3 * r:3 * r + 1] * o_c[r] + gt[:, 3 * r + 1:3 * r + 2] * o_s[r]
                    + gt[:, 3 * r + 2:3 * r + 3] * o_w[r])
    for j in range(NSA_REP // 2):
        o_ref[:, LANES * j:LANES * (j + 1)] = jnp.where(low, outs[2 * j], outs[2 * j + 1]).astype(BF16)


def _nsa_attention(nq, kcmp, vcmp, nkv, gates, batch, seq):
    tq, tk = NSA_TQ, NSA_TK
    nqt = seq // tq
    n_cmp_rows = kcmp.shape[2]
    n_slc = seq // SLC_BLOCK
    cmp_start = np.arange(n_cmp_rows) * CMP_STRIDE
    slc_start = np.arange(LANES) * SLC_BLOCK
    overlap = ((cmp_start[:, None] < slc_start[None, :] + SLC_BLOCK)
               & (cmp_start[:, None] + CMP_BLOCK > slc_start[None, :])
               & (np.arange(LANES)[None, :] < n_slc))
    ov = jnp.asarray(overlap, BF16)
    expand = jnp.asarray(np.arange(LANES)[:, None] == (np.arange(seq)[None, :] // SLC_BLOCK), BF16)
    gw = NSA_REP * HEAD_DIM
    kv_spec = lambda kind: pl.BlockSpec((seq, LANES), lambda b, g, i: (b, kind * NSA_KV_GROUPS + g))
    cmp_spec = pl.BlockSpec((None, None, n_cmp_rows, LANES), lambda b, g, i: (b, g, 0, 0))
    return pl.pallas_call(
        functools.partial(_nsa_kernel, tq=tq, tk=tk, seq=seq),
        grid=(batch, NSA_KV_GROUPS, nqt),
        in_specs=[pl.BlockSpec((tq, gw), lambda b, g, i: (b * nqt + i, g)),
                  cmp_spec, cmp_spec, kv_spec(0), kv_spec(1), kv_spec(2), kv_spec(3),
                  pl.BlockSpec((tq, LANES), lambda b, g, i: (b * nqt + i, g)),
                  pl.BlockSpec(ov.shape, lambda b, g, i: (0, 0)),
                  pl.BlockSpec(expand.shape, lambda b, g, i: (0, 0))],
        out_specs=pl.BlockSpec((tq, gw), lambda b, g, i: (b * nqt + i, g)),
        out_shape=jax.ShapeDtypeStruct((batch * seq, NSA_QW), BF16),
        compiler_params=_cparams(("arbitrary",) * 3),
        name="nsa_attention",
    )(nq, kcmp, vcmp, nkv, nkv, nkv, nkv, gates, ov, expand)


def _proj_residual_kernel(*refs, n_in):
    x_refs, w_refs = refs[:n_in], refs[n_in:2 * n_in]
    h_ref, g_ref, o_ref = refs[2 * n_in:]
    acc = None
    for x_ref, w_ref in zip(x_refs, w_refs):
        d = jnp.dot(x_ref[...], w_ref[...], preferred_element_type=F32)
        acc = d if acc is None else acc + d
    o_ref[...] = h_ref[...] + g_ref[...] * acc


def _proj_residual(xs, ws, h2, mod, layer, which, seq):
    t, d = h2.shape
    tm = ROW_TILE
    n_in = len(xs)
    row = lambda n: pl.BlockSpec((tm, n), lambda i: (i, 0))
    const = lambda shp: pl.BlockSpec(shp, lambda i: (0,) * len(shp))
    return pl.pallas_call(
        functools.partial(_proj_residual_kernel, n_in=n_in),
        grid=(t // tm,),
        in_specs=[row(x.shape[1]) for x in xs] + [const(w.shape) for w in ws] + [row(d)]
                 + _mod_specs(layer, (which,), seq // tm),
        out_specs=row(d),
        out_shape=jax.ShapeDtypeStruct((t, d), F32),
        compiler_params=_cparams(("arbitrary",)),
        name="proj_residual",
    )(*xs, *ws, h2, mod)


def _router_kernel(h_ref, gain_ref, sc_ref, sh_ref, w_ref, b_ref, tri_ref,
                   u_ref, e_ref, p_ref, r_ref, c_ref, cnt_ref):
    @pl.when(pl.program_id(0) == 0)
    def _():
        cnt_ref[...] = jnp.zeros_like(cnt_ref)

    u = _modulated_norm(h_ref[...], gain_ref[...], sc_ref[...], sh_ref[...])
    u_ref[...] = u
    logits = jnp.dot(u, w_ref[...], precision=HIGHEST, preferred_element_type=F32) + b_ref[...]
    tm = logits.shape[0]
    lane = lax.broadcasted_iota(I32, (tm, LANES), 1)
    lane_f = lane.astype(F32)
    work = logits
    tops, idxs, hots = [], [], []
    for _ in range(TOP_K):
        m = jnp.max(work, axis=1, keepdims=True)
        idx = jnp.min(jnp.where(work == m, lane_f, float(LANES)), axis=1, keepdims=True)
        hot = lane_f == idx
        work = jnp.where(hot, -jnp.inf, work)
        tops.append(m)
        idxs.append(idx)
        hots.append(hot)
    ex = [jnp.exp(m - tops[0]) for m in tops]
    den = ex[0] + ex[1] + ex[2] + ex[3]
    any_hot = jnp.zeros((tm, LANES), F32)
    for hot in hots:
        any_hot = any_hot + jnp.where(hot, 1.0, 0.0)
    before = jnp.dot(tri_ref[...], any_hot.astype(BF16), preferred_element_type=F32) + cnt_ref[...]
    e_out = jnp.zeros((tm, LANES), F32)
    p_out = jnp.zeros((tm, LANES), F32)
    r_out = jnp.zeros((tm, LANES), F32)
    for k in range(TOP_K):
        rank = jnp.sum(jnp.where(hots[k], before, 0.0), axis=1, keepdims=True)
        e_out = jnp.where(lane == k, idxs[k], e_out)
        p_out = jnp.where(lane == k, ex[k] / den, p_out)
        r_out = jnp.where(lane == k, rank, r_out)
    e_ref[...] = e_out.astype(I32)
    p_ref[...] = p_out
    r_ref[...] = r_out.astype(I32)
    cnt = cnt_ref[...] + jnp.sum(any_hot, axis=0, keepdims=True)
    cnt_ref[...] = cnt
    c_ref[...] = jnp.broadcast_to(cnt, c_ref.shape).astype(I32)


def _router(h2, mod, layer, gain, w_router, b_router, seq):
    t, d = h2.shape
    tm = ROW_TILE
    wp = jnp.pad(w_router, ((0, 0), (0, LANES - N_EXPERTS)))
    bp = jnp.pad(b_router, (0, LANES - N_EXPERTS), constant_values=NEG_BIG).reshape(1, LANES)
    r = np.arange(tm)
    tri = jnp.asarray(r[:, None] > r[None, :], BF16)
    row = lambda n: pl.BlockSpec((tm, n), lambda i: (i, 0))
    const = lambda shp: pl.BlockSpec(shp, lambda i: (0,) * len(shp))
    return pl.pallas_call(
        _router_kernel,
        grid=(t // tm,),
        in_specs=[row(d), const((1, d))] + _mod_specs(layer, (4, 3), seq // tm)
                 + [const((d, LANES)), const((1, LANES)), const((tm, tm))],
        out_specs=[row(d), row(LANES), row(LANES), row(LANES), const((8, LANES))],
        out_shape=[jax.ShapeDtypeStruct((t, d), F32), jax.ShapeDtypeStruct((t, LANES), I32),
                   jax.ShapeDtypeStruct((t, LANES), F32), jax.ShapeDtypeStruct((t, LANES), I32),
                   jax.ShapeDtypeStruct((8, LANES), I32)],
        scratch_shapes=[pltpu.VMEM((1, LANES), F32)],
        compiler_params=_cparams(("arbitrary",)),
        name="moe_router",
    )(h2, gain, mod, mod, wp, bp, tri)


def _expert_kernel(be_ref, nu_ref, st_ref, u_hbm, wgu_ref, bgu_ref, wd_ref, bd_ref, o_ref,
                   wgu_bf, wd_bf, xbuf, sem):
    i = pl.program_id(0)
    tm = xbuf.shape[1]
    n_used = nu_ref[0]
    slot = lax.rem(i, 2)

    def row_copy(block, buf, r):
        tok = st_ref[block * tm + r]
        return pltpu.make_async_copy(u_hbm.at[pl.ds(tok, 1)], xbuf.at[buf, pl.ds(r, 1)], sem.at[buf])

    def start_rows(block, buf):
        def body(r, c):
            row_copy(block, buf, r).start()
            return c
        lax.fori_loop(0, tm, body, 0, unroll=8)

    def wait_rows(block, buf):
        def body(r, c):
            row_copy(block, buf, r).wait()
            return c
        lax.fori_loop(0, tm, body, 0, unroll=8)

    @pl.when((i == 0) & (n_used > 0))
    def _():
        start_rows(0, 0)

    @pl.when(i + 1 < n_used)
    def _():
        start_rows(i + 1, 1 - slot)

    prev = be_ref[jnp.maximum(i - 1, 0)]
    fresh = (i == 0) | (be_ref[i] != prev)

    @pl.when(fresh & (i < n_used))
    def _():
        rows = 128

        def cast_gu(j, c):
            off = pl.multiple_of(j * rows, rows)
            wgu_bf[pl.ds(off, rows), :] = wgu_ref[pl.ds(off, rows), :].astype(BF16)
            wd_bf[pl.ds(off, rows), :] = wd_ref[pl.ds(off, rows), :].astype(BF16)
            return c

        lax.fori_loop(0, D_MODEL // rows, cast_gu, 0)

    @pl.when(i < n_used)
    def _():
        wait_rows(i, slot)
        x = xbuf[slot].astype(BF16)
        gu = jnp.dot(x, wgu_bf[...], preferred_element_type=F32) + bgu_ref[...]
        gate = jnp.minimum(gu[:, :D_EXPERT], SWIGLU_LIMIT)
        up = jnp.clip(gu[:, D_EXPERT:], -SWIGLU_LIMIT, SWIGLU_LIMIT)
        act = (up + 1.0) * gate * jax.nn.sigmoid(SWIGLU_ALPHA * gate)
        y = jnp.dot(act.astype(BF16), wd_bf[...], preferred_element_type=F32) + bd_ref[...]
        o_ref[...] = y.astype(o_ref.dtype)

    @pl.when(i >= n_used)
    def _():
        o_ref[...] = jnp.zeros_like(o_ref)


def _experts(u, slot_tok, blk_e, n_used, layer, w_gu, b_gu, w_down, b_down):
    n_slots = slot_tok.shape[0]
    d = u.shape[1]
    tm = MOE_TILE
    _, ne, _, dgu = w_gu.shape
    grid_spec = pltpu.PrefetchScalarGridSpec(
        num_scalar_prefetch=3,
        grid=(n_slots // tm,),
        in_specs=[pl.BlockSpec(memory_space=pl.ANY),
                  pl.BlockSpec((None, None, d, dgu), lambda i, be, nu, st: (layer, be[i], 0, 0)),
                  pl.BlockSpec((None, 1, dgu), lambda i, be, nu, st: (be[i], 0, 0)),
                  pl.BlockSpec((None, None, D_EXPERT, d), lambda i, be, nu, st: (layer, be[i], 0, 0)),
                  pl.BlockSpec((None, 1, d), lambda i, be, nu, st: (be[i], 0, 0))],
        out_specs=pl.BlockSpec((tm, d), lambda i, be, nu, st: (i, 0)),
        scratch_shapes=[pltpu.VMEM((d, dgu), BF16), pltpu.VMEM((D_EXPERT, d), BF16),
                        pltpu.VMEM((2, tm, d), F32), pltpu.SemaphoreType.DMA((2,))],
    )
    return pl.pallas_call(
        _expert_kernel,
        grid_spec=grid_spec,
        out_shape=jax.ShapeDtypeStruct((n_slots, d), BF16),
        compiler_params=_cparams(("arbitrary",)),
        name="moe_experts",
    )(blk_e, n_used, slot_tok, u, w_gu, b_gu.reshape(ne, 1, dgu), w_down, b_down.reshape(ne, 1, d))


def _combine_kernel(y_ref, p_ref, h_ref, g_ref, o_ref):
    p = p_ref[...]
    acc = p[:, 0:1] * y_ref[0].astype(F32)
    for k in range(1, TOP_K):
        acc = acc + p[:, k:k + 1] * y_ref[k].astype(F32)
    o_ref[...] = h_ref[...] + g_ref[...] * acc


def _combine(yg, top_p, h2, mod, layer, seq):
    t, d = h2.shape
    tm = ROW_TILE
    row = lambda n: pl.BlockSpec((tm, n), lambda i: (i, 0))
    return pl.pallas_call(
        _combine_kernel,
        grid=(t // tm,),
        in_specs=[pl.BlockSpec((TOP_K, tm, d), lambda i: (0, i, 0)), row(LANES), row(d)]
                 + _mod_specs(layer, (5,), seq // tm),
        out_specs=row(d),
        out_shape=jax.ShapeDtypeStruct((t, d), F32),
        compiler_params=_cparams(("arbitrary",)),
        name="moe_combine",
    )(yg, top_p, h2, mod)


def _moe(h2, mod, layer, gain, w_router, b_router, w_gu, b_gu, w_down, b_down, seq):
    t, d = h2.shape
    tm = MOE_TILE
    u, top_e, top_p, rank, counts = _router(h2, mod, layer, gain, w_router, b_router, seq)
    counts = counts[0, :N_EXPERTS]
    padded = (counts + tm - 1) // tm * tm
    pad_end = jnp.cumsum(padded)
    pad_start = pad_end - padded
    eids = jnp.arange(N_EXPERTS, dtype=I32)
    hot = top_e[:, :TOP_K, None] == eids
    slot = jnp.sum(jnp.where(hot, pad_start, 0), axis=-1) + rank[:, :TOP_K]
    n_slots = t * TOP_K + N_EXPERTS * tm
    tok = jnp.broadcast_to(jnp.arange(t, dtype=I32)[:, None], (t, TOP_K))
    slot_tok = jnp.zeros((n_slots,), I32).at[slot.reshape(-1)].set(tok.reshape(-1))
    n_blocks = n_slots // tm
    blk_row0 = jnp.arange(n_blocks, dtype=I32) * tm
    blk_e = jnp.minimum(jnp.sum((pad_end[None, :] <= blk_row0[:, None]).astype(I32), axis=1), N_EXPERTS - 1)
    n_used = (pad_end[-1] // tm).astype(I32).reshape(1)
    ys = _experts(u, slot_tok, blk_e, n_used, layer, w_gu, b_gu, w_down, b_down)
    yg = ys.at[slot.T.reshape(-1)].get(mode="promise_in_bounds").reshape(TOP_K, t, d)
    return _combine(yg, top_p, h2, mod, layer, seq)


SSM_Z0, SSM_X0, SSM_DT0, SSM_COLS = 0, SSM_INNER, SSM_INNER + SSM_CONV_DIM, SSM_INNER + SSM_CONV_DIM + SSM_GROUPS * LANES


def _ssm_weight_layout(w_in):
    parts = [w_in[:, :SSM_DT0]]
    for g in range(SSM_GROUPS):
        blk = w_in[:, SSM_DT0 + g * SSM_GROUP_HEADS: SSM_DT0 + (g + 1) * SSM_GROUP_HEADS]
        parts.append(jnp.pad(blk, ((0, 0), (0, LANES - SSM_GROUP_HEADS))))
    return jnp.concatenate(parts, axis=1).astype(BF16)


def _ssm_inproj_kernel(h_ref, gain_ref, sc_ref, sh_ref, w_ref, z_ref, xbc_ref, dt_ref):
    u = _modulated_norm(h_ref[...], gain_ref[...], sc_ref[...], sh_ref[...]).astype(BF16)
    step = 512
    for c in range(0, SSM_X0, step):
        z_ref[:, c:c + step] = jnp.dot(u, w_ref[:, c:c + step], preferred_element_type=F32)
    for c in range(0, SSM_CONV_DIM, step):
        xbc_ref[:, c:c + step] = jnp.dot(u, w_ref[:, SSM_X0 + c:SSM_X0 + c + step], preferred_element_type=F32)
    dt_ref[...] = jnp.dot(u, w_ref[:, SSM_DT0:SSM_COLS], preferred_element_type=F32)


def _ssm_inproj(h2, mod, layer, gain, w_bf, seq):
    t, d = h2.shape
    tm = ROW_TILE
    row = lambda n: pl.BlockSpec((tm, n), lambda i: (i, 0))
    const = lambda shp: pl.BlockSpec(shp, lambda i: (0,) * len(shp))
    return pl.pallas_call(
        _ssm_inproj_kernel,
        grid=(t // tm,),
        in_specs=[row(d), const((1, d))] + _mod_specs(layer, (1, 0), seq // tm) + [const((d, SSM_COLS))],
        out_specs=[row(SSM_INNER), row(SSM_CONV_DIM), row(SSM_GROUPS * LANES)],
        out_shape=[jax.ShapeDtypeStruct((t, SSM_INNER), F32), jax.ShapeDtypeStruct((t, SSM_CONV_DIM), F32),
                   jax.ShapeDtypeStruct((t, SSM_GROUPS * LANES), F32)],
        compiler_params=_cparams(("arbitrary",)),
        name="ssm_inproj",
    )(h2, gain, mod, mod, w_bf)


def _conv_kernel(x_ref, w_ref, b_ref, o_ref, ext_ref, *, ts):
    @pl.when(pl.program_id(2) == 0)
    def _():
        ext_ref[0:8, :] = jnp.zeros((8, ext_ref.shape[1]), F32)

    ext_ref[8:8 + ts, :] = x_ref[...]
    acc = b_ref[...] + w_ref[SSM_CONV - 1:SSM_CONV, :] * x_ref[...]
    for k in range(SSM_CONV - 1):
        acc = acc + w_ref[k:k + 1, :] * ext_ref[pl.ds(8 - (SSM_CONV - 1) + k, ts), :]
    o_ref[...] = acc * jax.nn.sigmoid(acc)
    ext_ref[0:8, :] = ext_ref[ts:ts + 8, :]


def _conv(xbc, conv_w, conv_b, batch, seq):
    t, c = xbc.shape
    ts, tc = CONV_ROWS, CONV_COLS
    ns = seq // ts
    return pl.pallas_call(
        functools.partial(_conv_kernel, ts=ts),
        grid=(c // tc, batch, ns),
        in_specs=[pl.BlockSpec((ts, tc), lambda j, b, i: (b * ns + i, j)),
                  pl.BlockSpec((SSM_CONV, tc), lambda j, b, i: (0, j)),
                  pl.BlockSpec((1, tc), lambda j, b, i: (0, j))],
        out_specs=pl.BlockSpec((ts, tc), lambda j, b, i: (b * ns + i, j)),
        out_shape=jax.ShapeDtypeStruct((t, c), F32),
        scratch_shapes=[pltpu.VMEM((ts + 8, tc), F32)],
        compiler_params=_cparams(("arbitrary",) * 3),
        name="ssm_conv",
    )(xbc, conv_w, conv_b.reshape(1, c))


def _ssd_kernel(x_ref, b_ref, c_ref, dt_ref, z_ref, dtb_ref, alog_ref, dsk_ref, ng_ref, tri_ref, ex_ref,
                o_ref, st_ref, *, chunk):
    @pl.when(pl.program_id(2) == 0)
    def _():
        st_ref[...] = jnp.zeros_like(st_ref)

    lane = lax.broadcasted_iota(I32, (1, LANES), 1)
    low = lane < SSM_HEAD_DIM
    x = x_ref[...]
    dt = _softplus(dt_ref[...] + dtb_ref[...])
    a = -jnp.exp(alog_ref[...])
    cs = _split_dot_left(tri_ref[...], dt * a, 3)
    ex = ex_ref[...]
    cs_x = _split_dot(cs, ex, 3)
    dt_x = _split_dot(dt, ex, 3)
    last_x = cs_x[chunk - 1:chunk, :]
    xd = x * dt_x
    bm = b_ref[...]
    cm = c_ref[...].astype(BF16)
    bt = bm.T.astype(BF16)
    cb = jnp.dot(cm, bt, preferred_element_type=F32)
    cs_t = cs.T
    row = lax.broadcasted_iota(I32, (chunk, chunk), 0)
    col = lax.broadcasted_iota(I32, (chunk, chunk), 1)
    causal = col <= row
    decay_in = jnp.exp(cs_x)
    w_end = jnp.exp(last_x - cs_x)
    decay_chunk = jnp.exp(last_x)
    ys = []
    for jp in range(SSM_GROUP_HEADS // 2):
        sl = slice(LANES * jp, LANES * (jp + 1))
        xd_pair = xd[:, sl]
        xd_bf = xd_pair.astype(BF16)
        yd = []
        for hh in range(2):
            j = 2 * jp + hh
            seg = cs[:, j:j + 1] - cs_t[j:j + 1, :]
            lm = jnp.exp(jnp.where(causal, seg, -jnp.inf))
            yd.append(jnp.dot((cb * lm).astype(BF16), xd_bf, preferred_element_type=F32))
        y = jnp.where(low, yd[0], yd[1])
        prev = st_ref[jp]
        y = y + jnp.dot(cm, prev.astype(BF16), preferred_element_type=F32) * decay_in[:, sl]
        new = jnp.dot(bt, (xd_pair * w_end[:, sl]).astype(BF16), preferred_element_type=F32)
        st_ref[jp] = prev * decay_chunk[:, sl] + new
        ys.append(y + x[:, sl] * dsk_ref[:, sl])
    y = jnp.concatenate(ys, axis=1)
    z = z_ref[...]
    y = y * (z * jax.nn.sigmoid(z))
    y = y * lax.rsqrt(jnp.mean(y * y, axis=-1, keepdims=True) + EPS)
    o_ref[...] = (y * ng_ref[...]).astype(BF16)


def _ssd(xbc, dt, z, dt_bias, a_log, d_skip, norm_g, batch, seq):
    t = xbc.shape[0]
    chunk = math.gcd(seq, SSM_CHUNK)
    nc = seq // chunk
    gw = SSM_GROUP_W
    xblocks = SSM_INNER // gw
    bblk0 = SSM_INNER // SSM_STATE
    cblk0 = bblk0 + SSM_GROUPS
    pad_g = lambda v: jnp.pad(v.reshape(SSM_GROUPS, 1, SSM_GROUP_HEADS),
                              ((0, 0), (0, 0), (0, LANES - SSM_GROUP_HEADS)))
    dsk = jnp.repeat(d_skip, SSM_HEAD_DIM).reshape(1, SSM_INNER)
    r = np.arange(chunk)
    tri = jnp.asarray(r[:, None] >= r[None, :], BF16)
    ex = jnp.asarray(np.arange(LANES)[:, None] == (np.arange(gw)[None, :] // SSM_HEAD_DIM), BF16)
    rowblk = lambda n, colfn: pl.BlockSpec((chunk, n), lambda b, g, c: (b * nc + c, colfn(g)))
    grp = pl.BlockSpec((None, 1, LANES), lambda b, g, c: (g, 0, 0))
    return pl.pallas_call(
        functools.partial(_ssd_kernel, chunk=chunk),
        grid=(batch, SSM_GROUPS, nc),
        in_specs=[rowblk(gw, lambda g: g), rowblk(SSM_STATE, lambda g: bblk0 + g),
                  rowblk(SSM_STATE, lambda g: cblk0 + g), rowblk(LANES, lambda g: g), rowblk(gw, lambda g: g),
                  grp, grp,
                  pl.BlockSpec((1, gw), lambda b, g, c: (0, g)), pl.BlockSpec((1, gw), lambda b, g, c: (0, g)),
                  pl.BlockSpec((chunk, chunk), lambda b, g, c: (0, 0)),
                  pl.BlockSpec((LANES, gw), lambda b, g, c: (0, 0))],
        out_specs=rowblk(gw, lambda g: g),
        out_shape=jax.ShapeDtypeStruct((t, SSM_INNER), BF16),
        scratch_shapes=[pltpu.VMEM((SSM_GROUP_HEADS // 2, SSM_STATE, LANES), F32)],
        compiler_params=_cparams(("arbitrary",) * 3),
        name="ssd",
    )(xbc, xbc, xbc, dt, z, pad_g(dt_bias), pad_g(a_log), dsk, norm_g.reshape(1, SSM_INNER), tri, ex)


def _attention_layer(h2, mod, layer, norm_mix, w_in, w_out, q_norm, k_norm, pe_k, pe_v, w1k, w2k, w1v, w2v,
                     batch, seq):
    tile2 = lambda v: jnp.concatenate([v, v]).reshape(1, LANES)
    bd = jnp.asarray(np.arange(LANES)[:, None] // HEAD_DIM == np.arange(LANES)[None, :] // HEAD_DIM, BF16)
    sb, nq, kvc, nkv, gates = _attn_inproj(h2, mod, layer, norm_mix.reshape(1, -1), _attn_weight_layout(w_in),
                                           tile2(q_norm), tile2(k_norm[1]), tile2(k_norm[2]), bd, seq)
    o_sb = _sb_attention(sb, batch, seq)
    kcmp, vcmp = _compress(kvc, batch, seq, pe_k, pe_v, w1k, w1v, w2k, w2v, tile2(k_norm[0]), bd)
    o_nsa = _nsa_attention(nq, kcmp, vcmp, nkv, gates, batch, seq)
    w_out_bf = w_out.astype(BF16)
    return _proj_residual([o_sb, o_nsa], [w_out_bf[:SB_W], w_out_bf[SB_W:]], h2, mod, layer, 2, seq)


def _mamba_layer(h2, mod, layer, norm_mix, w_in, conv_w, conv_b, dt_bias, a_log, d_skip, norm_g, w_out,
                 batch, seq):
    z, xbc, dt = _ssm_inproj(h2, mod, layer, norm_mix.reshape(1, -1), _ssm_weight_layout(w_in), seq)
    xbc = _conv(xbc, conv_w, conv_b, batch, seq)
    y = _ssd(xbc, dt, z, dt_bias, a_log, d_skip, norm_g, batch, seq)
    return _proj_residual([y], [w_out.astype(BF16)], h2, mod, layer, 2, seq)


def kernel(x, c, ada_w, ada_b, norm_mix, norm_ffn, attn_w_in, attn_w_out, nsa_q_norm, nsa_k_norm, cmp_pe_k, cmp_pe_v, cmp_w1_k, cmp_w2_k, cmp_w1_v, cmp_w2_v, ssm_w_in, ssm_conv_w, ssm_conv_b, ssm_dt_bias, ssm_a_log, ssm_d, ssm_norm, ssm_w_out, router_w, router_b, moe_w_gu, moe_b_gu, moe_w_down, moe_b_down):
    batch, seq, d = x.shape
    depth = ada_w.shape[0]
    c_pad = jnp.pad(c, ((0, 8 - batch % 8 if batch % 8 else 0), (0, 0)))
    mod = _adaln(c_pad, ada_w, ada_b)
    mod = mod.reshape(depth, c_pad.shape[0], 6, 1, d)
    h2 = x.reshape(batch * seq, d)
    for layer in range(depth):
        i = layer // 2
        if layer % 2 == 0:
            h2 = _attention_layer(h2, mod, layer, norm_mix[layer], attn_w_in[i], attn_w_out[i], nsa_q_norm[i],
                                  nsa_k_norm[i], cmp_pe_k[i], cmp_pe_v[i], cmp_w1_k[i], cmp_w2_k[i],
                                  cmp_w1_v[i], cmp_w2_v[i], batch, seq)
        else:
            h2 = _mamba_layer(h2, mod, layer, norm_mix[layer], ssm_w_in[i], ssm_conv_w[i], ssm_conv_b[i],
                              ssm_dt_bias[i], ssm_a_log[i], ssm_d[i], ssm_norm[i], ssm_w_out[i], batch, seq)
        h2 = _moe(h2, mod, layer, norm_ffn[layer].reshape(1, -1), router_w[layer], router_b[layer],
                  moe_w_gu, moe_b_gu[layer], moe_w_down, moe_b_down[layer], seq)
    return h2.reshape(batch, seq, d)
```
